```python
import math
import jax, jax.numpy as jnp
from jax import lax
import numpy as np

D_MODEL = 1024
BATCH = 4
SEQ = 4096
DEPTH = 4
DEC_BATCH = 32
DEC_SEQ = 4
PAST_LEN = 8192
PAGE_SIZE = 128

HEAD_DIM = 64
FOX_HEADS = 8
MOBA_HEADS = 8
FOX_WIDTH = FOX_HEADS * HEAD_DIM
MOBA_WIDTH = MOBA_HEADS * HEAD_DIM
ATTN_WIDTH = FOX_WIDTH + MOBA_WIDTH
ATTN_SCALE = HEAD_DIM ** -0.5
FOX_Q_BLOCK = 128
FORGET_BIAS_INIT = 3.0
MOBA_BLOCK = 256
MOBA_TOPK = 3
MOBA_Q_CHUNK = 64
POOL_WINDOWS = (2, 4, 8, 16)
POOL_GROUPS = 4
POOL_WIDTH = 512
POOL_GROUP_WIDTH = POOL_WIDTH // POOL_GROUPS
POOL_CTX = 15
GDN_HEADS = 4
GDN_HEAD_DIM = 128
GDN_WIDTH = GDN_HEADS * GDN_HEAD_DIM
GDN_SCALE = GDN_HEAD_DIM ** -0.5
GDN_CHUNK = 64
CONV_WIDTH = 4
CONV_CTX = CONV_WIDTH - 1
ODD_MIX_WIDTH = POOL_WIDTH + GDN_WIDTH
D_FF = ((8 * D_MODEL + 3 * 256 - 1) // (3 * 256)) * 256
N_EVEN = (DEPTH + 1) // 2
N_ODD = DEPTH // 2
DEEPNORM_ALPHA = (2 * DEPTH) ** 0.25
DEEPNORM_BETA = (8 * DEPTH) ** -0.25
EVEN_SPLITS = (FOX_WIDTH, FOX_WIDTH, FOX_WIDTH, FOX_HEADS, MOBA_WIDTH, MOBA_WIDTH, MOBA_WIDTH)
ODD_SPLITS = (POOL_WIDTH, 3 * GDN_WIDTH, GDN_HEADS, GDN_HEADS, GDN_WIDTH)
EVEN_IN_WIDTH = 3 * FOX_WIDTH + FOX_HEADS + 3 * MOBA_WIDTH
ODD_IN_WIDTH = POOL_WIDTH + 4 * GDN_WIDTH + 2 * GDN_HEADS
LN_EPS = 1e-5
NORM_EPS = 1e-6
NEG_INF = -1e30
F32 = jnp.float32

kernel_name = 'hybrid_fox_moba_pool_gdn_step'


def _split(x, sizes):
    offs = np.cumsum(np.array(sizes))[:-1].tolist()
    return jnp.split(x, offs, axis=-1)


def layer_norm(x, g, b):
    xf = x.astype(F32)
    xc = xf - xf.mean(-1, keepdims=True)
    var = (xc * xc).mean(-1, keepdims=True)
    return (xc * lax.rsqrt(var + LN_EPS) * g.astype(F32) + b.astype(F32)).astype(x.dtype)


def l2_normalize(x):
    return x * lax.rsqrt((x * x).sum(-1, keepdims=True) + NORM_EPS)


def swiglu(h, w_gate_up, w_down):
    g, u = jnp.split(h @ w_gate_up, 2, axis=-1)
    return (jax.nn.silu(g) * u) @ w_down


def fox_attend(q, c_q, q_pos, k, v, c_k, k_pos):
    s = jnp.einsum('nthd,nlhd->nhtl', q, k).astype(F32) * ATTN_SCALE
    s = s + jnp.swapaxes(c_q, 1, 2)[..., :, None] - jnp.swapaxes(c_k, 1, 2)[..., None, :]
    s = jnp.where(k_pos[None, :] <= q_pos[:, None], s, NEG_INF)
    p = jax.nn.softmax(s, axis=-1)
    return jnp.einsum('nhtl,nlhd->nthd', p.astype(v.dtype), v)


def fox_attention(q, k, v, logf, q_block):
    N, T, H, Dh = q.shape
    L = k.shape[1]
    c = jnp.cumsum(logf.astype(F32), axis=1)
    k_pos = jnp.arange(L)
    nb = T // q_block
    qs = q.reshape(N, nb, q_block, H, Dh).swapaxes(0, 1)
    cs = c[:, L - T:].reshape(N, nb, q_block, H).swapaxes(0, 1)
    ps = jnp.arange(L - T, L).reshape(nb, q_block)
    out = lax.map(lambda a: fox_attend(a[0], a[1], a[2], k, v, c, k_pos), (qs, cs, ps))
    return out.swapaxes(0, 1).reshape(N, T, H * Dh)


def to_blocks(x):
    N, L, H, Dh = x.shape
    nb = -(-L // MOBA_BLOCK)
    x = jnp.pad(x, ((0, 0), (0, nb * MOBA_BLOCK - L), (0, 0), (0, 0)))
    return x.reshape(N, nb, MOBA_BLOCK, H, Dh)


def moba_attend(q, q_pos, kb, vb, kmean):
    N, T, H, Dh = q.shape
    nb = kb.shape[1]
    own = q_pos // MOBA_BLOCK
    gate = jnp.einsum('nthd,nbhd->nhtb', q.astype(F32), kmean)
    gate = jnp.where(jnp.arange(nb)[None, :] < own[:, None], gate, NEG_INF)
    _, sel = lax.top_k(gate, min(MOBA_TOPK, nb))
    sel_ok = sel < own[:, None]
    own_b = jnp.broadcast_to(own[None, None, :, None], (N, H, T, 1)).astype(sel.dtype)
    idx = jnp.concatenate([sel, own_b], axis=-1)
    ok = jnp.concatenate([sel_ok, jnp.ones(own_b.shape, bool)], axis=-1)
    n_ix = jnp.arange(N)[:, None, None, None]
    h_ix = jnp.arange(H)[None, :, None, None]
    k_sel = kb[n_ix, idx, :, h_ix]
    v_sel = vb[n_ix, idx, :, h_ix]
    s = jnp.einsum('nthd,nhtrkd->nhtrk', q, k_sel).astype(F32) * ATTN_SCALE
    key_pos = idx[..., None] * MOBA_BLOCK + jnp.arange(MOBA_BLOCK)
    valid = ok[..., None] & (key_pos <= q_pos[:, None, None])
    s = jnp.where(valid, s, NEG_INF)
    p = jax.nn.softmax(s.reshape(N, H, T, -1), axis=-1).reshape(s.shape)
    return jnp.einsum('nhtrk,nhtrkd->nthd', p.astype(v_sel.dtype), v_sel)


def moba_attention(q, k, v, q_chunk):
    N, T, H, Dh = q.shape
    L = k.shape[1]
    kb, vb = to_blocks(k), to_blocks(v)
    kmean = kb.astype(F32).mean(axis=2)
    nq = T // q_chunk
    qs = q.reshape(N, nq, q_chunk, H, Dh).swapaxes(0, 1)
    ps = jnp.arange(L - T, L).reshape(nq, q_chunk)
    out = lax.map(lambda a: moba_attend(a[0], a[1], kb, vb, kmean), (qs, ps))
    return out.swapaxes(0, 1).reshape(N, T, H * Dh)


def even_split(h, w_in, b_f):
    N, L, _ = h.shape
    fq, fk, fv, fl, mq, mk, mv = _split(h @ w_in, EVEN_SPLITS)
    hd = lambda t, nh: t.reshape(N, L, nh, HEAD_DIM)
    logf = jax.nn.log_sigmoid(fl.astype(F32) + b_f.astype(F32))
    return (hd(fq, FOX_HEADS), hd(fk, FOX_HEADS), hd(fv, FOX_HEADS), logf,
            hd(mq, MOBA_HEADS), hd(mk, MOBA_HEADS), hd(mv, MOBA_HEADS))


def pool_mix(u_ext, pos, w_pool, pool_scale):
    N, LC, W = u_ext.shape
    L = LC - POOL_CTX
    uf = u_ext.astype(F32)
    cs = jnp.concatenate([jnp.zeros((N, 1, W), F32), jnp.cumsum(uf, axis=1)], axis=1)
    hi = cs[:, POOL_CTX + 1:]
    parts = []
    for gi, w in enumerate(POOL_WINDOWS):
        sl = slice(gi * POOL_GROUP_WIDTH, (gi + 1) * POOL_GROUP_WIDTH)
        lo = cs[:, POOL_CTX + 1 - w:POOL_CTX + 1 - w + L, sl]
        cnt = jnp.minimum(w, pos + 1).astype(F32)
        parts.append((hi[..., sl] - lo) / cnt[None, :, None])
    d = jnp.concatenate(parts, axis=-1) - uf[:, POOL_CTX:]
    y = jnp.einsum('ntgc,gce->ntge', d.reshape(N, L, POOL_GROUPS, POOL_GROUP_WIDTH), w_pool.astype(F32))
    return (y.reshape(N, L, POOL_WIDTH) * pool_scale.astype(F32)).astype(u_ext.dtype)


def short_conv(x_ext, w):
    L = x_ext.shape[1] - CONV_CTX
    y = x_ext[:, 0:L] * w[0]
    for i in range(1, CONV_WIDTH):
        y = y + x_ext[:, i:i + L] * w[i]
    return jax.nn.silu(y)


def gated_delta_rule(q, k, v, g, beta, s0, chunk):
    N, L, H, Dk = q.shape
    Dv = v.shape[-1]
    nc = L // chunk
    c5 = lambda x: x.reshape(N, nc, chunk, H, x.shape[-1]).transpose(1, 0, 3, 2, 4)
    c4 = lambda x: x.reshape(N, nc, chunk, H).transpose(1, 0, 3, 2)
    qc, kc, vc = c5(q), c5(k), c5(v)
    gcum = jnp.cumsum(c4(g), axis=-1)
    bc = c4(beta)
    ii = jnp.arange(chunk)
    incl = ii[:, None] >= ii[None, :]
    strict = ii[:, None] > ii[None, :]
    diff = gcum[..., :, None] - gcum[..., None, :]
    decay = jnp.where(incl, jnp.exp(jnp.where(incl, diff, 0.0)), 0.0)
    kbeta = kc * bc[..., None]
    a = jnp.where(strict, jnp.einsum('...id,...jd->...ij', kbeta, kc) * decay, 0.0)
    eye = jnp.eye(chunk, dtype=a.dtype)
    tmat = lax.linalg.triangular_solve(eye + a, jnp.broadcast_to(eye, a.shape),
                                       left_side=True, lower=True, unit_diagonal=True)
    u = tmat @ (vc * bc[..., None])
    w = tmat @ (kbeta * jnp.exp(gcum)[..., None])
    intra = jnp.where(incl, jnp.einsum('...id,...jd->...ij', qc, kc) * decay, 0.0)

    def step(s, inp):
        q_i, k_i, u_i, w_i, g_i, a_i = inp
        v_new = u_i - w_i @ s
        o = (q_i * jnp.exp(g_i)[..., None]) @ s + a_i @ v_new
        g_last = g_i[..., -1:]
        s = s * jnp.exp(g_last)[..., None] + jnp.einsum(
            'nhcd,nhce->nhde', k_i * jnp.exp(g_last - g_i)[..., None], v_new)
        return s, o

    s_fin, o = lax.scan(step, s0, (qc, kc, u, w, gcum, intra))
    return o.transpose(1, 0, 3, 2, 4).reshape(N, L, H, Dv), s_fin


def odd_mixer(h, pool_ctx, conv_ctx, s0, pos, chunk, w_in, conv_w, a_log, dt_bias,
              gdn_norm_w, w_pool, pool_scale, w_out):
    N, L, _ = h.shape
    u, qkv, a, b, z = _split(h @ w_in, ODD_SPLITS)
    u_ext = jnp.concatenate([pool_ctx.astype(u.dtype), u], axis=1)
    y_pool = pool_mix(u_ext, pos, w_pool, pool_scale)
    qkv_ext = jnp.concatenate([conv_ctx.astype(qkv.dtype), qkv], axis=1)
    q, k, v = jnp.split(short_conv(qkv_ext, conv_w).astype(F32), 3, axis=-1)
    q = l2_normalize(q.reshape(N, L, GDN_HEADS, GDN_HEAD_DIM)) * GDN_SCALE
    k = l2_normalize(k.reshape(N, L, GDN_HEADS, GDN_HEAD_DIM))
    v = v.reshape(N, L, GDN_HEADS, GDN_HEAD_DIM)
    g = -jnp.exp(a_log.astype(F32)) * jax.nn.softplus(a.astype(F32) + dt_bias.astype(F32))
    beta = jax.nn.sigmoid(b.astype(F32))
    o, s_new = gated_delta_rule(q, k, v, g, beta, s0.astype(F32), chunk)
    o = o * lax.rsqrt((o * o).mean(-1, keepdims=True) + NORM_EPS) * gdn_norm_w.astype(F32)
    o = o * jax.nn.silu(z.astype(F32).reshape(N, L, GDN_HEADS, GDN_HEAD_DIM))
    mixed = jnp.concatenate([y_pool, o.reshape(N, L, GDN_WIDTH).astype(h.dtype)], axis=-1) @ w_out
    return mixed, u_ext[:, -POOL_CTX:], qkv_ext[:, -CONV_CTX:], s_new


def setup_inputs(seed: int = 0) -> dict:
    key = jax.random.key(seed)
    ks = list(jax.random.split(key, 40))
    nrm = lambda shape, scale=1.0: scale * jax.random.normal(ks.pop(), shape, F32)
    n_pages = PAST_LEN // PAGE_SIZE
    used = DEC_BATCH * n_pages
    pool_pages = used + (used + 3) // 4
    page_table = jax.random.permutation(ks.pop(), pool_pages)[:used].reshape(DEC_BATCH, n_pages).astype(jnp.int32)
    kv_shape = (N_EVEN, pool_pages, PAGE_SIZE, FOX_HEADS, HEAD_DIM)
    mkv_shape = (N_EVEN, pool_pages, PAGE_SIZE, MOBA_HEADS, HEAD_DIM)
    dt = jnp.exp(jax.random.uniform(ks.pop(), (N_ODD, GDN_HEADS), F32, math.log(1e-3), math.log(1e-1)))
    return {
        'x_prompt': nrm((BATCH, SEQ, D_MODEL)),
        'x_sample': nrm((DEC_BATCH, DEC_SEQ, D_MODEL)),
        'cache_fox_k': nrm(kv_shape),
        'cache_fox_v': nrm(kv_shape),
        'cache_fox_logf': jax.nn.log_sigmoid(FORGET_BIAS_INIT + nrm(kv_shape[:4])),
        'cache_moba_k': nrm(mkv_shape),
        'cache_moba_v': nrm(mkv_shape),
        'state_pool': nrm((N_ODD, DEC_BATCH, POOL_CTX, POOL_WIDTH)),
        'state_conv': nrm((N_ODD, DEC_BATCH, CONV_CTX, 3 * GDN_WIDTH)),
        'state_gdn': nrm((N_ODD, DEC_BATCH, GDN_HEADS, GDN_HEAD_DIM, GDN_HEAD_DIM), 0.1),
        'page_table': page_table,
        'w_in_even': nrm((N_EVEN, D_MODEL, EVEN_IN_WIDTH), D_MODEL ** -0.5),
        'b_forget': FORGET_BIAS_INIT + nrm((N_EVEN, FOX_HEADS), 0.1),
        'w_out_even': nrm((N_EVEN, ATTN_WIDTH, D_MODEL), DEEPNORM_BETA * ATTN_WIDTH ** -0.5),
        'w_in_odd': nrm((N_ODD, D_MODEL, ODD_IN_WIDTH), D_MODEL ** -0.5),
        'conv_w': nrm((N_ODD, CONV_WIDTH, 3 * GDN_WIDTH), CONV_WIDTH ** -0.5),
        'a_log': jnp.log(jax.random.uniform(ks.pop(), (N_ODD, GDN_HEADS), F32, 1.0, 16.0)),
        'dt_bias': dt + jnp.log(-jnp.expm1(-dt)),
        'gdn_norm_w': 1.0 + nrm((N_ODD, GDN_HEAD_DIM), 0.02),
        'w_pool': nrm((N_ODD, POOL_GROUPS, POOL_GROUP_WIDTH, POOL_GROUP_WIDTH), POOL_GROUP_WIDTH ** -0.5),
        'pool_scale': 1.0 + nrm((N_ODD, POOL_WIDTH), 0.05),
        'w_out_odd': nrm((N_ODD, ODD_MIX_WIDTH, D_MODEL), DEEPNORM_BETA * ODD_MIX_WIDTH ** -0.5),
        'ln1_g': 1.0 + nrm((DEPTH, D_MODEL), 0.02),
        'ln1_b': nrm((DEPTH, D_MODEL), 0.02),
        'ln2_g': 1.0 + nrm((DEPTH, D_MODEL), 0.02),
        'ln2_b': nrm((DEPTH, D_MODEL), 0.02),
        'w_gate_up': nrm((DEPTH, D_MODEL, 2 * D_FF), D_MODEL ** -0.5),
        'w_down': nrm((DEPTH, D_FF, D_MODEL), DEEPNORM_BETA * D_FF ** -0.5),
    }


def reference(x_prompt, x_sample, cache_fox_k, cache_fox_v, cache_fox_logf, cache_moba_k, cache_moba_v,
              state_pool, state_conv, state_gdn, page_table,
              w_in_even, b_forget, w_out_even, w_in_odd, conv_w, a_log, dt_bias, gdn_norm_w,
              w_pool, pool_scale, w_out_odd, ln1_g, ln1_b, ln2_g, ln2_b, w_gate_up, w_down):
    B, S, _ = x_prompt.shape
    DB, T, _ = x_sample.shape
    P = page_table.shape[1] * PAGE_SIZE
    pos_p = jnp.arange(S)
    pos_s = P + jnp.arange(T)
    hp, hs = x_prompt, x_sample

    def past_rows(cache, li):
        rows = cache[li, page_table]
        return rows.reshape((DB, P) + rows.shape[3:])

    fk_p, fv_p, fl_p, mk_p, mv_p = [], [], [], [], []
    fk_s, fv_s, fl_s, mk_s, mv_s = [], [], [], [], []
    pool_p, conv_p, gdn_p, pool_s, conv_s, gdn_s = [], [], [], [], [], []
    for layer in range(DEPTH):
        li = layer // 2
        if layer % 2 == 0:
            fq, fk, fv, fl, mq, mk, mv = even_split(hp, w_in_even[li], b_forget[li])
            att = jnp.concatenate([fox_attention(fq, fk, fv, fl, FOX_Q_BLOCK),
                                   moba_attention(mq, mk, mv, MOBA_Q_CHUNK)], axis=-1)
            mix_p = att @ w_out_even[li]
            fk_p.append(fk); fv_p.append(fv); fl_p.append(fl); mk_p.append(mk); mv_p.append(mv)
            fq, fk, fv, fl, mq, mk, mv = even_split(hs, w_in_even[li], b_forget[li])
            fk_all = jnp.concatenate([past_rows(cache_fox_k, li).astype(fk.dtype), fk], axis=1)
            fv_all = jnp.concatenate([past_rows(cache_fox_v, li).astype(fv.dtype), fv], axis=1)
            fl_all = jnp.concatenate([past_rows(cache_fox_logf, li).astype(F32), fl], axis=1)
            fox_out = fox_attention(fq, fk_all, fv_all, fl_all, T)
            mk_all = jnp.concatenate([past_rows(cache_moba_k, li).astype(mk.dtype), mk], axis=1)
            mv_all = jnp.concatenate([past_rows(cache_moba_v, li).astype(mv.dtype), mv], axis=1)
            moba_out = moba_attention(mq, mk_all, mv_all, T)
            mix_s = jnp.concatenate([fox_out, moba_out], axis=-1) @ w_out_even[li]
            fk_s.append(fk); fv_s.append(fv); fl_s.append(fl); mk_s.append(mk); mv_s.append(mv)
        else:
            odd_w = (w_in_odd[li], conv_w[li], a_log[li], dt_bias[li], gdn_norm_w[li],
                     w_pool[li], pool_scale[li], w_out_odd[li])
            mix_p, sp, sc, sg = odd_mixer(
                hp, jnp.zeros((B, POOL_CTX, POOL_WIDTH), hp.dtype),
                jnp.zeros((B, CONV_CTX, 3 * GDN_WIDTH), hp.dtype),
                jnp.zeros((B, GDN_HEADS, GDN_HEAD_DIM, GDN_HEAD_DIM), F32),
                pos_p, GDN_CHUNK, *odd_w)
            pool_p.append(sp); conv_p.append(sc); gdn_p.append(sg)
            mix_s, sp, sc, sg = odd_mixer(hs, state_pool[li], state_conv[li], state_gdn[li],
                                          pos_s, T, *odd_w)
            pool_s.append(sp); conv_s.append(sc); gdn_s.append(sg)
        hp = layer_norm(DEEPNORM_ALPHA * hp + mix_p, ln1_g[layer], ln1_b[layer])
        hp = layer_norm(DEEPNORM_ALPHA * hp + swiglu(hp, w_gate_up[layer], w_down[layer]), ln2_g[layer], ln2_b[layer])
        hs = layer_norm(DEEPNORM_ALPHA * hs + mix_s, ln1_g[layer], ln1_b[layer])
        hs = layer_norm(DEEPNORM_ALPHA * hs + swiglu(hs, w_gate_up[layer], w_down[layer]), ln2_g[layer], ln2_b[layer])

    return (hp, hs,
            jnp.stack(fk_p), jnp.stack(fv_p), jnp.stack(fl_p), jnp.stack(mk_p), jnp.stack(mv_p),
            jnp.stack(pool_p), jnp.stack(conv_p), jnp.stack(gdn_p),
            jnp.stack(fk_s), jnp.stack(fv_s), jnp.stack(fl_s), jnp.stack(mk_s), jnp.stack(mv_s),
            jnp.stack(pool_s), jnp.stack(conv_s), jnp.stack(gdn_s))
```

```python
import functools

import jax
import jax.numpy as jnp
from jax import lax
from jax.experimental import pallas as pl
from jax.experimental.pallas import tpu as pltpu

F32 = jnp.float32
BF16 = jnp.bfloat16
HI = lax.Precision.HIGHEST

D_MODEL = 1024
HEAD_DIM = 64
FOX_HEADS = 8
MOBA_HEADS = 8
ATTN_HALF = FOX_HEADS * HEAD_DIM
ATTN_SCALE = HEAD_DIM ** -0.5
MOBA_BLOCK = 256
MOBA_TOPK = 3
PAGE_SIZE = 128
POOL_WINDOWS = (2, 4, 8, 16)
POOL_WIDTH = 512
POOL_GROUP_WIDTH = 128
POOL_CTX = 15
GDN_HEADS = 4
GDN_HEAD_DIM = 128
GDN_WIDTH = GDN_HEADS * GDN_HEAD_DIM
GDN_SCALE = GDN_HEAD_DIM ** -0.5
GDN_CHUNK = 64
CONV_WIDTH = 4
CONV_CTX = CONV_WIDTH - 1
DEPTH = 4
DEEPNORM_ALPHA = (2 * DEPTH) ** 0.25
LN_EPS = 1e-5
NORM_EPS = 1e-6
NEG_INF = -1e30

LANES = 128
SUBLANES = 8
POOL_HALO = 16
CONV_HALO = 8
VMEM_LIMIT = 56 * 1024 * 1024


def _cparams(n_axes):
    return pltpu.CompilerParams(dimension_semantics=("arbitrary",) * n_axes,
                                vmem_limit_bytes=VMEM_LIMIT)


def _dot(a, b, precision=lax.Precision.DEFAULT):
    return jnp.dot(a, b, preferred_element_type=F32, precision=precision)


def _dot_split3(a, b):
    hi = a.astype(BF16)
    mid = (a - hi.astype(F32)).astype(BF16)
    lo = (a - hi.astype(F32) - mid.astype(F32)).astype(BF16)
    return _dot(hi, b) + _dot(mid, b) + _dot(lo, b)


def _dot_nt(a, b, precision=lax.Precision.DEFAULT):
    return lax.dot_general(a, b, (((1,), (1,)), ((), ())), preferred_element_type=F32, precision=precision)


def _dot_tn(a, b, precision=lax.Precision.DEFAULT):
    return lax.dot_general(a, b, (((0,), (0,)), ((), ())), preferred_element_type=F32, precision=precision)


def _sigmoid(x):
    return 1.0 / (1.0 + jnp.exp(-x))


def _softplus(x):
    return jnp.maximum(x, 0.0) + jnp.log1p(jnp.exp(-jnp.abs(x)))


def _layer_norm(y, g, b):
    mu = jnp.mean(y, axis=-1, keepdims=True)
    yc = y - mu
    var = jnp.mean(yc * yc, axis=-1, keepdims=True)
    return yc * lax.rsqrt(var + LN_EPS) * g + b


def _iota(shape, axis):
    return lax.broadcasted_iota(jnp.int32, shape, axis)


def _proj_kernel(x_ref, w_ref, ws_ref, bs_ref, *out_refs, widths, log_sigmoid):
    xb = x_ref[...].astype(BF16)
    off = 0
    for o_ref, wd in zip(out_refs[:-1], widths):
        o_ref[...] = _dot(xb, w_ref[:, off:off + wd])
        off += wd
    small = _dot(xb, ws_ref[...]) + bs_ref[...]
    if log_sigmoid:
        small = -_softplus(-small)
    out_refs[-1][...] = small


def _proj(x, w_main, w_small, b_small, widths, log_sigmoid):
    m = x.shape[0]
    tm = min(512, m)
    n_main = w_main.shape[1]
    outs = [jax.ShapeDtypeStruct((m, wd), F32) for wd in widths] + [jax.ShapeDtypeStruct((m, LANES), F32)]
    return pl.pallas_call(
        functools.partial(_proj_kernel, widths=widths, log_sigmoid=log_sigmoid),
        grid=(m // tm,),
        in_specs=[pl.BlockSpec((tm, D_MODEL), lambda i: (i, 0)),
                  pl.BlockSpec((D_MODEL, n_main), lambda i: (0, 0)),
                  pl.BlockSpec((D_MODEL, LANES), lambda i: (0, 0)),
                  pl.BlockSpec((1, LANES), lambda i: (0, 0))],
        out_specs=[pl.BlockSpec((tm, wd), lambda i: (i, 0)) for wd in widths]
        + [pl.BlockSpec((tm, LANES), lambda i: (i, 0))],
        out_shape=outs,
        compiler_params=_cparams(1),
        name="in_proj",
    )(x, w_main, w_small, b_small)


def _outproj_ln_kernel(a1_ref, a2_ref, x_ref, w1_ref, w2_ref, g_ref, b_ref, o_ref):
    mix = _dot(a1_ref[...].astype(BF16), w1_ref[...]) + _dot(a2_ref[...].astype(BF16), w2_ref[...])
    o_ref[...] = _layer_norm(DEEPNORM_ALPHA * x_ref[...] + mix, g_ref[...], b_ref[...])


def _outproj_ln(a1, a2, x, w1, w2, g, b):
    m = x.shape[0]
    tm = min(512, m)
    k1, k2 = a1.shape[1], a2.shape[1]
    return pl.pallas_call(
        _outproj_ln_kernel,
        grid=(m // tm,),
        in_specs=[pl.BlockSpec((tm, k1), lambda i: (i, 0)),
                  pl.BlockSpec((tm, k2), lambda i: (i, 0)),
                  pl.BlockSpec((tm, D_MODEL), lambda i: (i, 0)),
                  pl.BlockSpec((k1, D_MODEL), lambda i: (0, 0)),
                  pl.BlockSpec((k2, D_MODEL), lambda i: (0, 0)),
                  pl.BlockSpec((1, D_MODEL), lambda i: (0, 0)),
                  pl.BlockSpec((1, D_MODEL), lambda i: (0, 0))],
        out_specs=pl.BlockSpec((tm, D_MODEL), lambda i: (i, 0)),
        out_shape=jax.ShapeDtypeStruct((m, D_MODEL), F32),
        compiler_params=_cparams(1),
        name="outproj_ln",
    )(a1, a2, x, w1, w2, g, b)


def _ffn_kernel(x_ref, wg_ref, wu_ref, wd_ref, g_ref, b_ref, o_ref, xb_ref, acc_ref):
    j = pl.program_id(1)

    @pl.when(j == 0)
    def _():
        xb_ref[...] = x_ref[...].astype(BF16)
        acc_ref[...] = jnp.zeros_like(acc_ref)

    xb = xb_ref[...]
    gate = _dot(xb, wg_ref[...])
    up = _dot(xb, wu_ref[...])
    act = (gate * _sigmoid(gate) * up).astype(BF16)
    acc_ref[...] += _dot(act, wd_ref[...])

    @pl.when(j == pl.num_programs(1) - 1)
    def _():
        o_ref[...] = _layer_norm(DEEPNORM_ALPHA * x_ref[...] + acc_ref[...], g_ref[...], b_ref[...])


def _ffn_ln(x, w_gate_up, w_down, g, b):
    m = x.shape[0]
    d_ff = w_down.shape[0]
    tm = min(1024, m)
    tf = 256
    nf = d_ff // tf
    return pl.pallas_call(
        _ffn_kernel,
        grid=(m // tm, nf),
        in_specs=[pl.BlockSpec((tm, D_MODEL), lambda i, j: (i, 0)),
                  pl.BlockSpec((D_MODEL, tf), lambda i, j: (0, j)),
                  pl.BlockSpec((D_MODEL, tf), lambda i, j: (0, j + nf)),
                  pl.BlockSpec((tf, D_MODEL), lambda i, j: (j, 0)),
                  pl.BlockSpec((1, D_MODEL), lambda i, j: (0, 0)),
                  pl.BlockSpec((1, D_MODEL), lambda i, j: (0, 0))],
        out_specs=pl.BlockSpec((tm, D_MODEL), lambda i, j: (i, 0)),
        out_shape=jax.ShapeDtypeStruct((m, D_MODEL), F32),
        scratch_shapes=[pltpu.VMEM((tm, D_MODEL), BF16), pltpu.VMEM((tm, D_MODEL), F32)],
        compiler_params=_cparams(2),
        name="ffn_ln",
    )(x, w_gate_up, w_gate_up, w_down, g, b)


def _cumsum_kernel(x_ref, o_ref, *, blk):
    n = x_ref.shape[-1]
    tri = (_iota((blk, blk), 0) <= _iota((blk, blk), 1)).astype(F32)
    carry = jnp.zeros((x_ref.shape[1], 1), F32)
    for i in range(n // blk):
        y = _dot(x_ref[0, :, i * blk:(i + 1) * blk], tri, HI) + carry
        o_ref[0, :, i * blk:(i + 1) * blk] = y
        carry = y[:, blk - 1:blk]


def _cumsum_rows(x):
    b, h, s = x.shape
    return pl.pallas_call(
        functools.partial(_cumsum_kernel, blk=min(256, s)),
        grid=(b,),
        in_specs=[pl.BlockSpec((1, h, s), lambda i: (i, 0, 0))],
        out_specs=pl.BlockSpec((1, h, s), lambda i: (i, 0, 0)),
        out_shape=jax.ShapeDtypeStruct((b, h, s), F32),
        compiler_params=_cparams(1),
        name="logf_cumsum",
    )(x)


def _softmax_step(s, valid, h, m_ref, l_ref):
    if valid is not None:
        s = jnp.where(valid, s, NEG_INF)
    m_old = m_ref[h]
    m_new = jnp.maximum(m_old, jnp.max(s, axis=1, keepdims=True))
    alpha = jnp.exp(m_old - m_new)
    p = jnp.exp(s - m_new)
    if valid is not None:
        p = jnp.where(valid, p, 0.0)
    l_ref[h] = alpha * l_ref[h] + jnp.sum(p, axis=1, keepdims=True)
    m_ref[h] = m_new
    return p, alpha


def _prompt_attn_kernel(q_ref, k_ref, v_ref, c_ref, o_ref, kb_ref, vb_ref, km_ref, sel_ref,
                        m_ref, l_ref, acc_ref, *, t, moba):
    qi = pl.program_id(2)
    seq = k_ref.shape[1]
    nblk = seq // MOBA_BLOCK

    @pl.when(qi == 0)
    def _():
        kb_ref[...] = k_ref[0].astype(BF16)
        vb_ref[...] = v_ref[0].astype(BF16)
        if moba:
            km_ref[...] = jnp.zeros_like(km_ref)
            for blk in range(nblk):
                rows = k_ref[0, blk * MOBA_BLOCK:(blk + 1) * MOBA_BLOCK, :]
                km_ref[blk:blk + 1, :] = jnp.sum(rows, axis=0, keepdims=True) * (1.0 / MOBA_BLOCK)

    lane = _iota((t, LANES), 1)
    head0 = lane < HEAD_DIM
    row = _iota((t, t), 0)
    col = _iota((t, t), 1)
    q0 = pl.multiple_of(qi * t, t)
    q = q_ref[0]
    q_heads = (jnp.where(head0, q, 0.0), jnp.where(head0, 0.0, q))
    qb = tuple((qh * ATTN_SCALE).astype(BF16) for qh in q_heads)

    if moba:
        for h in range(2):
            gate = _dot_nt(q_heads[h], km_ref[...], HI)
            gate = jnp.where(lane < qi, gate, NEG_INF)
            rank = jnp.zeros((t, LANES), jnp.int32)
            for j in range(nblk):
                gj = gate[:, j:j + 1]
                ahead = (gj > gate) | ((gj == gate) & (lane > j))
                rank = rank + ahead.astype(jnp.int32)
            sel_ref[h] = jnp.where((rank < MOBA_TOPK) & (lane < qi), 1.0, 0.0)
        cq = None
    else:
        cq = tuple(jnp.sum(jnp.where(row == col, c_ref[0, 0, h:h + 1, pl.ds(q0, t)], 0.0),
                           axis=1, keepdims=True) for h in range(2))

    m_ref[...] = jnp.full(m_ref.shape, NEG_INF, F32)
    l_ref[...] = jnp.zeros_like(l_ref)
    acc_ref[...] = jnp.zeros_like(acc_ref)

    def tile(ki, k0, diagonal):
        kt = kb_ref[pl.ds(k0, t), :]
        vt = vb_ref[pl.ds(k0, t), :]
        pvs, alphas = [], []
        for h in range(2):
            s = _dot_nt(qb[h], kt)
            if moba:
                if diagonal:
                    valid = col <= row
                else:
                    picked = jnp.sum(jnp.where(lane == ki, sel_ref[h], 0.0), axis=1, keepdims=True)
                    valid = picked > 0.5
            else:
                s = s + cq[h] - c_ref[0, 0, h:h + 1, pl.ds(k0, t)]
                valid = (col <= row) if diagonal else None
            p, alpha = _softmax_step(s, valid, h, m_ref, l_ref)
            pvs.append(_dot(p.astype(BF16), vt))
            alphas.append(alpha)
        acc_ref[...] = (acc_ref[...] * jnp.where(head0, alphas[0], alphas[1])
                        + jnp.where(head0, pvs[0], pvs[1]))

    def body(ki, carry):
        tile(ki, pl.multiple_of(ki * t, t), False)
        return carry

    lax.fori_loop(0, qi, body, 0)
    tile(qi, q0, True)
    o_ref[0] = acc_ref[...] / jnp.where(head0, l_ref[0], l_ref[1])


def _prompt_attn(q, k, v, c_rows, moba):
    b, s, _ = q.shape
    t = MOBA_BLOCK
    n_pairs = ATTN_HALF // LANES
    kernel = functools.partial(_prompt_attn_kernel, t=t, moba=moba)
    return pl.pallas_call(
        kernel,
        grid=(b, n_pairs, s // t),
        in_specs=[pl.BlockSpec((1, t, LANES), lambda i, p, j: (i, j, p)),
                  pl.BlockSpec((1, s, LANES), lambda i, p, j: (i, 0, p)),
                  pl.BlockSpec((1, s, LANES), lambda i, p, j: (i, 0, p)),
                  pl.BlockSpec((1, 1, 2, s), lambda i, p, j: (i, p, 0, 0))],
        out_specs=pl.BlockSpec((1, t, LANES), lambda i, p, j: (i, j, p)),
        out_shape=jax.ShapeDtypeStruct((b, s, ATTN_HALF), F32),
        scratch_shapes=[pltpu.VMEM((s, LANES), BF16), pltpu.VMEM((s, LANES), BF16),
                        pltpu.VMEM((LANES, LANES), F32), pltpu.VMEM((2, t, LANES), F32),
                        pltpu.VMEM((2, t, 1), F32), pltpu.VMEM((2, t, 1), F32),
                        pltpu.VMEM((t, LANES), F32)],
        compiler_params=_cparams(3),
        name="moba_prompt" if moba else "fox_prompt",
    )(q, k, v, c_rows)


PAGES_PER_STEP = 4
ROWS = 32


def _block_diag_queries(q):
    t = q.shape[0]
    rep = jnp.concatenate([jnp.broadcast_to(q[i:i + 1], (FOX_HEADS, ATTN_HALF)) for i in range(t)], axis=0)
    keep = (_iota(rep.shape, 1) // HEAD_DIM) == (_iota(rep.shape, 0) % FOX_HEADS)
    return jnp.where(keep, rep, 0.0)


def _heads_to_tokens(o):
    keep = (_iota(o.shape, 1) // HEAD_DIM) == (_iota(o.shape, 0) % FOX_HEADS)
    o = jnp.where(keep, o, 0.0)
    t = o.shape[0] // FOX_HEADS
    return jnp.sum(o.reshape(t, FOX_HEADS, ATTN_HALF), axis=1)


def _fox_decode_kernel(pt_ref, q_ref, kn_ref, vn_ref, ln_ref, *refs, g):
    k_refs, v_refs, l_refs = refs[:g], refs[g:2 * g], refs[2 * g:3 * g]
    o_ref = refs[3 * g]
    qb_ref, m_ref, l_ref, acc_ref, carry_ref, rowc_ref = refs[3 * g + 1:]
    j = pl.program_id(1)
    lane = _iota((ROWS, PAGE_SIZE), 1)
    tok = _iota((ROWS, PAGE_SIZE), 0) // FOX_HEADS
    later = (_iota((PAGE_SIZE, PAGE_SIZE), 0) > _iota((PAGE_SIZE, PAGE_SIZE), 1)).astype(F32).astype(BF16)

    def page(kp, vp, lt, new):
        d8 = _dot_split3(lt, later) + carry_ref[...]
        carry_ref[...] = carry_ref[...] + jnp.sum(lt, axis=1, keepdims=True)
        d = jnp.concatenate([d8] * (ROWS // FOX_HEADS), axis=0)
        if new:
            rowc_ref[...] = jnp.sum(jnp.where(lane == tok, d, 0.0), axis=1, keepdims=True)
        s = _dot_nt(qb_ref[...], kp.astype(BF16)) + d - rowc_ref[...]
        p, alpha = _softmax_step(s, (lane <= tok) if new else None, 0, m_ref, l_ref)
        acc_ref[...] = acc_ref[...] * alpha + _dot(p.astype(BF16), vp.astype(BF16))

    @pl.when(j == 0)
    def _():
        qb_ref[...] = (_block_diag_queries(q_ref[0]) * ATTN_SCALE).astype(BF16)
        m_ref[...] = jnp.full(m_ref.shape, NEG_INF, F32)
        l_ref[...] = jnp.zeros_like(l_ref)
        acc_ref[...] = jnp.zeros_like(acc_ref)
        carry_ref[...] = jnp.zeros_like(carry_ref)
        page(kn_ref[0], vn_ref[0], ln_ref[0], True)

    for i in range(g):
        page(k_refs[i][0, 0], v_refs[i][0, 0], l_refs[i][0, 0], False)

    @pl.when(j == pl.num_programs(1) - 1)
    def _():
        o_ref[0] = _heads_to_tokens(acc_ref[...] / l_ref[0])


def _fox_decode(page_table, li, q, k_new, v_new, l_new, cache_k, cache_v, cache_lt):
    db, t, _ = q.shape
    n_pages = page_table.shape[1]
    g = PAGES_PER_STEP
    steps = n_pages // g

    def page_map(i, n, j, pt):
        return (li, pt[n, n_pages - 1 - (j * g + i)], 0, 0)

    sample = lambda n, j, pt: (n, 0, 0)
    in_specs = [pl.BlockSpec((1, t, ATTN_HALF), sample),
                pl.BlockSpec((1, PAGE_SIZE, ATTN_HALF), sample),
                pl.BlockSpec((1, PAGE_SIZE, ATTN_HALF), sample),
                pl.BlockSpec((1, FOX_HEADS, PAGE_SIZE), sample)]
    in_specs += [pl.BlockSpec((1, 1, PAGE_SIZE, ATTN_HALF), functools.partial(page_map, i)) for i in range(g)] * 2
    in_specs += [pl.BlockSpec((1, 1, FOX_HEADS, PAGE_SIZE), functools.partial(page_map, i)) for i in range(g)]
    grid_spec = pltpu.PrefetchScalarGridSpec(
        num_scalar_prefetch=1, grid=(db, steps), in_specs=in_specs,
        out_specs=pl.BlockSpec((1, t, ATTN_HALF), sample),
        scratch_shapes=[pltpu.VMEM((ROWS, ATTN_HALF), BF16), pltpu.VMEM((1, ROWS, 1), F32),
                        pltpu.VMEM((1, ROWS, 1), F32), pltpu.VMEM((ROWS, ATTN_HALF), F32),
                        pltpu.VMEM((FOX_HEADS, 1), F32), pltpu.VMEM((ROWS, 1), F32)])
    return pl.pallas_call(
        functools.partial(_fox_decode_kernel, g=g),
        grid_spec=grid_spec,
        out_shape=jax.ShapeDtypeStruct((db, t, ATTN_HALF), F32),
        compiler_params=_cparams(2),
        name="fox_decode",
    )(page_table, q, k_new, v_new, l_new, *([cache_k] * g), *([cache_v] * g), *([cache_lt] * g))


def _moba_decode_kernel(pt_ref, q_ref, kn_ref, vn_ref, *refs, g, n_blocks):
    k_refs, v_refs = refs[:g], refs[g:2 * g]
    o_ref = refs[2 * g]
    qf_ref, qb_ref, mb_ref, lb_ref, gb_ref, ob_ref, m_ref, l_ref = refs[2 * g + 1:]
    j = pl.program_id(1)
    lane = _iota((ROWS, LANES), 1)
    tok = _iota((ROWS, LANES), 0) // MOBA_HEADS
    pages_per_block = MOBA_BLOCK // PAGE_SIZE

    @pl.when(j == 0)
    def _():
        qf = _block_diag_queries(q_ref[0])
        qf_ref[...] = qf
        qb_ref[...] = (qf * ATTN_SCALE).astype(BF16)
        mb_ref[...] = jnp.zeros_like(mb_ref)
        lb_ref[...] = jnp.zeros_like(lb_ref)
        gb_ref[...] = jnp.zeros_like(gb_ref)

    for bl in range(g // pages_per_block):
        blk = j * (g // pages_per_block) + bl
        ks = [k_refs[bl * pages_per_block + i][0, 0] for i in range(pages_per_block)]
        vs = [v_refs[bl * pages_per_block + i][0, 0] for i in range(pages_per_block)]
        ksum = sum(jnp.sum(kp, axis=0, keepdims=True) for kp in ks)
        gate = jnp.sum(qf_ref[...] * (ksum * (1.0 / MOBA_BLOCK)), axis=1, keepdims=True)
        ss = [_dot_nt(qb_ref[...], kp.astype(BF16)) for kp in ks]
        m_b = functools.reduce(jnp.maximum, [jnp.max(s, axis=1, keepdims=True) for s in ss])
        ps = [jnp.exp(s - m_b) for s in ss]
        l_b = sum(jnp.sum(p, axis=1, keepdims=True) for p in ps)
        ob_ref[blk] = sum(_dot(p.astype(BF16), vp.astype(BF16)) for p, vp in zip(ps, vs))
        here = lane == blk
        mb_ref[...] = jnp.where(here, m_b, mb_ref[...])
        lb_ref[...] = jnp.where(here, l_b, lb_ref[...])
        gb_ref[...] = jnp.where(here, gate, gb_ref[...])

    @pl.when(j == pl.num_programs(1) - 1)
    def _():
        m_ref[...] = jnp.full(m_ref.shape, NEG_INF, F32)
        l_ref[...] = jnp.zeros_like(l_ref)
        s_own = _dot_nt(qb_ref[...], kn_ref[0].astype(BF16))
        p_own, _ = _softmax_step(s_own, lane <= tok, 0, m_ref, l_ref)
        o_own = _dot(p_own.astype(BF16), vn_ref[0].astype(BF16))
        m_own, l_own = m_ref[0], l_ref[0]
        valid = lane < n_blocks
        gate = jnp.where(valid, gb_ref[...], NEG_INF)
        rank = jnp.zeros((ROWS, LANES), jnp.int32)
        for b in range(n_blocks):
            gj = gate[:, b:b + 1]
            rank = rank + ((gj > gate) | ((gj == gate) & (lane > b))).astype(jnp.int32)
        sel = (rank < MOBA_TOPK) & valid
        m_all = jnp.maximum(jnp.max(jnp.where(sel, mb_ref[...], NEG_INF), axis=1, keepdims=True), m_own)
        wgt = jnp.where(sel, jnp.exp(jnp.where(sel, mb_ref[...] - m_all, 0.0)), 0.0)
        w_own = jnp.exp(m_own - m_all)
        den = jnp.sum(wgt * lb_ref[...], axis=1, keepdims=True) + w_own * l_own
        num = w_own * o_own
        for b in range(n_blocks):
            num = num + wgt[:, b:b + 1] * ob_ref[b]
        o_ref[0] = _heads_to_tokens(num / den)


def _moba_decode(page_table, li, q, k_new, v_new, cache_k, cache_v):
    db, t, _ = q.shape
    n_pages = page_table.shape[1]
    g = PAGES_PER_STEP
    steps = n_pages // g
    n_blocks = n_pages * PAGE_SIZE // MOBA_BLOCK

    def page_map(i, n, j, pt):
        return (li, pt[n, j * g + i], 0, 0)

    sample = lambda n, j, pt: (n, 0, 0)
    in_specs = [pl.BlockSpec((1, t, ATTN_HALF), sample),
                pl.BlockSpec((1, PAGE_SIZE, ATTN_HALF), sample),
                pl.BlockSpec((1, PAGE_SIZE, ATTN_HALF), sample)]
    in_specs += [pl.BlockSpec((1, 1, PAGE_SIZE, ATTN_HALF), functools.partial(page_map, i)) for i in range(g)] * 2
    grid_spec = pltpu.PrefetchScalarGridSpec(
        num_scalar_prefetch=1, grid=(db, steps), in_specs=in_specs,
        out_specs=pl.BlockSpec((1, t, ATTN_HALF), sample),
        scratch_shapes=[pltpu.VMEM((ROWS, ATTN_HALF), F32), pltpu.VMEM((ROWS, ATTN_HALF), BF16),
                        pltpu.VMEM((ROWS, LANES), F32), pltpu.VMEM((ROWS, LANES), F32),
                        pltpu.VMEM((ROWS, LANES), F32), pltpu.VMEM((n_blocks, ROWS, ATTN_HALF), F32),
                        pltpu.VMEM((1, ROWS, 1), F32), pltpu.VMEM((1, ROWS, 1), F32)])
    return pl.pallas_call(
        functools.partial(_moba_decode_kernel, g=g, n_blocks=n_blocks),
        grid_spec=grid_spec,
        out_shape=jax.ShapeDtypeStruct((db, t, ATTN_HALF), F32),
        compiler_params=_cparams(2),
        name="moba_decode",
    )(page_table, q, k_new, v_new, *([cache_k] * g), *([cache_v] * g))


def _pool_kernel(u_ref, ctx_ref, w_ref, sc_ref, o_ref, ext_ref, *, tm, pos0):
    i = pl.program_id(1)

    @pl.when(i == 0)
    def _():
        ext_ref[0:POOL_HALO, :] = ctx_ref[0]

    @pl.when(i > 0)
    def _():
        ext_ref[0:POOL_HALO, :] = ext_ref[tm:tm + POOL_HALO, :]

    ext_ref[POOL_HALO:, :] = u_ref[0]
    pos = pos0 + i * tm + _iota((tm, 1), 0)
    for gi, win in enumerate(POOL_WINDOWS):
        cols = slice(gi * POOL_GROUP_WIDTH, (gi + 1) * POOL_GROUP_WIDTH)
        x = ext_ref[:, cols]
        tot = x
        shift = 1
        while shift < win:
            tot = tot + pltpu.roll(tot, shift, 0)
            shift *= 2
        cnt = jnp.minimum(win, pos + 1).astype(F32)
        d = tot[POOL_HALO:] / cnt - x[POOL_HALO:]
        y = _dot(d.astype(BF16), w_ref[gi])
        o_ref[0, :, cols] = y * sc_ref[:, cols]


def _pool_mix(u, ctx, w_pool, pool_scale, pos0):
    n, l, _ = u.shape
    tm = min(512, l)
    return pl.pallas_call(
        functools.partial(_pool_kernel, tm=tm, pos0=pos0),
        grid=(n, l // tm),
        in_specs=[pl.BlockSpec((1, tm, POOL_WIDTH), lambda b, i: (b, i, 0)),
                  pl.BlockSpec((1, POOL_HALO, POOL_WIDTH), lambda b, i: (b, 0, 0)),
                  pl.BlockSpec((len(POOL_WINDOWS), POOL_GROUP_WIDTH, POOL_GROUP_WIDTH), lambda b, i: (0, 0, 0)),
                  pl.BlockSpec((1, POOL_WIDTH), lambda b, i: (0, 0))],
        out_specs=pl.BlockSpec((1, tm, POOL_WIDTH), lambda b, i: (b, i, 0)),
        out_shape=jax.ShapeDtypeStruct((n, l, POOL_WIDTH), F32),
        scratch_shapes=[pltpu.VMEM((POOL_HALO + tm, POOL_WIDTH), F32)],
        compiler_params=_cparams(2),
        name="pool_mix",
    )(u, ctx, w_pool, pool_scale)


def _gdn_prep_kernel(x_ref, ctx_ref, ab_ref, cw_ref, alog_ref, dtb_ref, q_ref, k_ref, v_ref, gb_ref, ext_ref, *, tm):
    i = pl.program_id(1)

    @pl.when(i == 0)
    def _():
        ext_ref[0:CONV_HALO, :] = ctx_ref[0]

    @pl.when(i > 0)
    def _():
        ext_ref[0:CONV_HALO, :] = ext_ref[tm:tm + CONV_HALO, :]

    ext_ref[CONV_HALO:, :] = x_ref[0]
    for part, out_ref in enumerate((q_ref, k_ref, v_ref)):
        for h in range(GDN_HEADS):
            c0 = part * GDN_WIDTH + h * GDN_HEAD_DIM
            x = ext_ref[:, c0:c0 + GDN_HEAD_DIM]
            y = x[CONV_HALO:] * cw_ref[CONV_WIDTH - 1:CONV_WIDTH, c0:c0 + GDN_HEAD_DIM]
            for tap in range(CONV_WIDTH - 1):
                shifted = pltpu.roll(x, CONV_CTX - tap, 0)[CONV_HALO:]
                y = y + shifted * cw_ref[tap:tap + 1, c0:c0 + GDN_HEAD_DIM]
            y = y * _sigmoid(y)
            if part < 2:
                y = y * lax.rsqrt(jnp.sum(y * y, axis=-1, keepdims=True) + NORM_EPS)
            if part == 0:
                y = y * GDN_SCALE
            out_ref[0, :, h * GDN_HEAD_DIM:(h + 1) * GDN_HEAD_DIM] = y
    ab = ab_ref[0]
    g = -jnp.exp(alog_ref[...]) * _softplus(ab + dtb_ref[...])
    beta = _sigmoid(ab)
    gb_ref[0] = jnp.where(_iota(ab.shape, 1) < GDN_HEADS, g, beta)


def _gdn_prep(qkv, ctx, ab, conv_w, a_log_row, dt_bias_row):
    n, l, width = qkv.shape
    tm = min(256, l)
    outs = [jax.ShapeDtypeStruct((n, l, GDN_WIDTH), F32)] * 3 + [jax.ShapeDtypeStruct((n, l, LANES), F32)]
    row = lambda b, i: (b, i, 0)
    fixed = lambda b, i: (0, 0)
    return pl.pallas_call(
        functools.partial(_gdn_prep_kernel, tm=tm),
        grid=(n, l // tm),
        in_specs=[pl.BlockSpec((1, tm, width), row),
                  pl.BlockSpec((1, CONV_HALO, width), lambda b, i: (b, 0, 0)),
                  pl.BlockSpec((1, tm, LANES), row),
                  pl.BlockSpec((CONV_WIDTH, width), fixed),
                  pl.BlockSpec((1, LANES), fixed),
                  pl.BlockSpec((1, LANES), fixed)],
        out_specs=[pl.BlockSpec((1, tm, GDN_WIDTH), row)] * 3 + [pl.BlockSpec((1, tm, LANES), row)],
        out_shape=outs,
        scratch_shapes=[pltpu.VMEM((CONV_HALO + tm, width), F32)],
        compiler_params=_cparams(2),
        name="gdn_prep",
    )(qkv, ctx, ab, conv_w, a_log_row, dt_bias_row)


def _gdn_chunk_kernel(q_ref, k_ref, v_ref, gb_ref, u_ref, w_ref, qg_ref, kd_ref, at_ref, eg_ref, *, c, chunks):
    ri = _iota((c, c), 0)
    ci = _iota((c, c), 1)
    eye = ri == ci
    incl = ri >= ci
    strict = ri > ci
    eye_f = eye.astype(F32)
    levels = max(1, (c - 1).bit_length())
    for ch in range(chunks):
        rows = slice(ch * c, (ch + 1) * c)
        gb = gb_ref[0, rows, :]
        eg_full = jnp.zeros((c, LANES), F32)
        for h in range(GDN_HEADS):
            cols = slice(h * GDN_HEAD_DIM, (h + 1) * GDN_HEAD_DIM)
            q, k, v = q_ref[0, rows, cols], k_ref[0, rows, cols], v_ref[0, rows, cols]
            g_col = gb[:, h:h + 1]
            beta = gb[:, GDN_HEADS + h:GDN_HEADS + h + 1]
            g_row = jnp.sum(jnp.where(eye, g_col, 0.0), axis=0, keepdims=True)
            gc_col = jnp.sum(jnp.where(incl, g_row, 0.0), axis=1, keepdims=True)
            gc_row = jnp.sum(jnp.where(ri <= ci, g_col, 0.0), axis=0, keepdims=True)
            decay = jnp.where(incl, jnp.exp(jnp.where(incl, gc_col - gc_row, 0.0)), 0.0)
            kbeta = k * beta
            a = jnp.where(strict, _dot_nt(kbeta, k, HI) * decay, 0.0)
            t = eye_f - a
            npow = _dot(a, a, HI)
            for lvl in range(1, levels):
                t = t + _dot(t, npow, HI)
                if lvl < levels - 1:
                    npow = _dot(npow, npow, HI)
            eg = jnp.exp(gc_col)
            g_last = gc_col[c - 1:c, :]
            u_ref[0, rows, cols] = _dot(t, v * beta, HI)
            w_ref[0, rows, cols] = _dot(t, kbeta * eg, HI)
            qg_ref[0, rows, cols] = q * eg
            kd_ref[0, rows, cols] = k * jnp.exp(g_last - gc_col)
            at_ref[0, rows, h * c:(h + 1) * c] = jnp.where(incl, _dot_nt(q, k, HI) * decay, 0.0)
            eg_full = jnp.where(_iota((c, LANES), 1) == h, eg, eg_full)
        eg_ref[0, rows, :] = eg_full


def _gdn_chunks(q, k, v, gb, c):
    n, l, _ = q.shape
    chunks = max(1, min(4, l // c))
    tm = chunks * c
    row = lambda b, i: (b, i, 0)
    big = jax.ShapeDtypeStruct((n, l, GDN_WIDTH), F32)
    outs = [big] * 4 + [jax.ShapeDtypeStruct((n, l, GDN_HEADS * c), F32), jax.ShapeDtypeStruct((n, l, LANES), F32)]
    return pl.pallas_call(
        functools.partial(_gdn_chunk_kernel, c=c, chunks=chunks),
        grid=(n, l // tm),
        in_specs=[pl.BlockSpec((1, tm, GDN_WIDTH), row)] * 3 + [pl.BlockSpec((1, tm, LANES), row)],
        out_specs=[pl.BlockSpec((1, tm, GDN_WIDTH), row)] * 4
        + [pl.BlockSpec((1, tm, GDN_HEADS * c), row), pl.BlockSpec((1, tm, LANES), row)],
        out_shape=outs,
        compiler_params=_cparams(2),
        name="gdn_chunks",
    )(q, k, v, gb)


def _gdn_scan_kernel(u_ref, w_ref, qg_ref, kd_ref, at_ref, eg_ref, z_ref, s0_ref, nw_ref, o_ref, sout_ref, s_ref, *, c):
    i = pl.program_id(1)

    @pl.when(i == 0)
    def _():
        s_ref[...] = s0_ref[0]

    for h in range(GDN_HEADS):
        cols = slice(h * GDN_HEAD_DIM, (h + 1) * GDN_HEAD_DIM)
        s = s_ref[h]
        v_new = u_ref[0, :, cols] - _dot(w_ref[0, :, cols], s, HI)
        o = _dot(qg_ref[0, :, cols], s, HI) + _dot(at_ref[0, :, h * c:(h + 1) * c], v_new, HI)
        s_ref[h] = s * eg_ref[0, c - 1:c, h:h + 1] + _dot_tn(kd_ref[0, :, cols], v_new, HI)
        o = o * lax.rsqrt(jnp.mean(o * o, axis=-1, keepdims=True) + NORM_EPS) * nw_ref[...]
        z = z_ref[0, :, cols]
        o_ref[0, :, cols] = o * (z * _sigmoid(z))

    @pl.when(i == pl.num_programs(1) - 1)
    def _():
        sout_ref[0] = s_ref[...]


def _gdn_scan(u, w, qg, kd, at, eg, z, s0, norm_w, c):
    n, l, _ = u.shape
    row = lambda b, i: (b, i, 0)
    state = lambda b, i: (b, 0, 0, 0)
    st_shape = (1, GDN_HEADS, GDN_HEAD_DIM, GDN_HEAD_DIM)
    return pl.pallas_call(
        functools.partial(_gdn_scan_kernel, c=c),
        grid=(n, l // c),
        in_specs=[pl.BlockSpec((1, c, GDN_WIDTH), row)] * 4
        + [pl.BlockSpec((1, c, GDN_HEADS * c), row), pl.BlockSpec((1, c, LANES), row),
           pl.BlockSpec((1, c, GDN_WIDTH), row), pl.BlockSpec(st_shape, state),
           pl.BlockSpec((1, GDN_HEAD_DIM), lambda b, i: (0, 0))],
        out_specs=[pl.BlockSpec((1, c, GDN_WIDTH), row), pl.BlockSpec(st_shape, state)],
        out_shape=[jax.ShapeDtypeStruct((n, l, GDN_WIDTH), F32),
                   jax.ShapeDtypeStruct((n,) + st_shape[1:], F32)],
        scratch_shapes=[pltpu.VMEM(st_shape[1:], F32)],
        compiler_params=_cparams(2),
        name="gdn_scan",
    )(u, w, qg, kd, at, eg, z, s0, norm_w)


def _pad_lanes(x, width=LANES):
    return jnp.pad(x, [(0, 0)] * (x.ndim - 1) + [(0, width - x.shape[-1])])


def _even_weights(w_in, b_f):
    fq, fk, fv, fl, mq, mk, mv = jnp.split(
        w_in, [ATTN_HALF, 2 * ATTN_HALF, 3 * ATTN_HALF, 3 * ATTN_HALF + FOX_HEADS,
               4 * ATTN_HALF + FOX_HEADS, 5 * ATTN_HALF + FOX_HEADS], axis=1)
    w_main = jnp.concatenate([fq, fk, fv, mq, mk, mv], axis=1).astype(BF16)
    return w_main, _pad_lanes(fl).astype(BF16), _pad_lanes(b_f[None, :])


def _odd_weights(w_in):
    u, qkv, a, b, z = jnp.split(
        w_in, [POOL_WIDTH, POOL_WIDTH + 3 * GDN_WIDTH, POOL_WIDTH + 3 * GDN_WIDTH + GDN_HEADS,
               POOL_WIDTH + 3 * GDN_WIDTH + 2 * GDN_HEADS], axis=1)
    w_main = jnp.concatenate([u, qkv, z], axis=1).astype(BF16)
    return w_main, _pad_lanes(jnp.concatenate([a, b], axis=1)).astype(BF16)


def _odd_mixer(h, n, l, pool_ctx, conv_ctx, s0, pos0, chunk, w_main, w_small, conv_w, a_log_row, dt_bias_row,
               norm_w, w_pool, pool_scale):
    u, qkv, z, ab = _proj(h, w_main, w_small, jnp.zeros((1, LANES), F32),
                          (POOL_WIDTH, 3 * GDN_WIDTH, GDN_WIDTH), False)
    u3, qkv3, z3, ab3 = (t.reshape(n, l, -1) for t in (u, qkv, z, ab))
    lp = -(-l // SUBLANES) * SUBLANES
    padr = lambda t: jnp.pad(t, ((0, 0), (0, lp - l), (0, 0)))
    ctx_p = jnp.pad(pool_ctx, ((0, 0), (POOL_HALO - POOL_CTX, 0), (0, 0)))
    ctx_c = jnp.pad(conv_ctx, ((0, 0), (CONV_HALO - CONV_CTX, 0), (0, 0)))
    y_pool = _pool_mix(padr(u3), ctx_p, w_pool, pool_scale, pos0)[:, :l]
    q, k, v, gb = _gdn_prep(padr(qkv3), ctx_c, padr(ab3), conv_w, a_log_row, dt_bias_row)
    if lp != l:
        gb = gb * (jnp.arange(lp) < l).astype(F32)[None, :, None]
    if chunk % SUBLANES:
        assert chunk == l
        chunk = lp
    c = chunk
    uu, ww, qg, kd, at, eg = _gdn_chunks(q, k, v, gb, c)
    o, s_new = _gdn_scan(uu, ww, qg, kd, at, eg, padr(z3), s0, norm_w, c)
    new_pool = jnp.concatenate([pool_ctx, u3], axis=1)[:, -POOL_CTX:]
    new_conv = jnp.concatenate([conv_ctx, qkv3], axis=1)[:, -CONV_CTX:]
    return (y_pool.reshape(n * l, POOL_WIDTH), o[:, :l].reshape(n * l, GDN_WIDTH), new_pool, new_conv, s_new)


def kernel(x_prompt, x_sample, cache_fox_k, cache_fox_v, cache_fox_logf, cache_moba_k, cache_moba_v, state_pool, state_conv, state_gdn, page_table, w_in_even, b_forget, w_out_even, w_in_odd, conv_w, a_log, dt_bias, gdn_norm_w, w_pool, pool_scale, w_out_odd, ln1_g, ln1_b, ln2_g, ln2_b, w_gate_up, w_down):
    B, S, _ = x_prompt.shape
    DB, T, _ = x_sample.shape
    n_even, pool_pages = cache_fox_k.shape[:2]
    P = page_table.shape[1] * PAGE_SIZE
    hp = x_prompt.reshape(B * S, D_MODEL)
    hs = x_sample.reshape(DB * T, D_MODEL)

    paged = lambda c: c.reshape(n_even, pool_pages, PAGE_SIZE, ATTN_HALF)
    cfk, cfv, cmk, cmv = paged(cache_fox_k), paged(cache_fox_v), paged(cache_moba_k), paged(cache_moba_v)
    cfl_t = jnp.swapaxes(cache_fox_logf, 2, 3)
    pad_keys = lambda t: jnp.pad(t.reshape(DB, T, ATTN_HALF), ((0, 0), (0, PAGE_SIZE - T), (0, 0)))

    outs_p = {k: [] for k in ("fk", "fv", "fl", "mk", "mv", "pool", "conv", "gdn")}
    outs_s = {k: [] for k in ("fk", "fv", "fl", "mk", "mv", "pool", "conv", "gdn")}
    widths_even = (ATTN_HALF,) * 6
    for layer in range(DEPTH):
        li = layer // 2
        row = lambda a: a[layer][None, :]
        if layer % 2 == 0:
            w_main, w_fl, b_fl = _even_weights(w_in_even[li], b_forget[li])
            w_out = w_out_even[li].astype(BF16)
            fq, fk, fv, mq, mk, mv, fl = _proj(hp, w_main, w_fl, b_fl, widths_even, True)
            b3 = lambda t: t.reshape(B, S, ATTN_HALF)
            logf = fl[:, :FOX_HEADS].reshape(B, S, FOX_HEADS)
            c_rows = _cumsum_rows(jnp.swapaxes(logf, 1, 2)).reshape(B, FOX_HEADS // 2, 2, S)
            fox = _prompt_attn(b3(fq), b3(fk), b3(fv), c_rows, False)
            moba = _prompt_attn(b3(mq), b3(mk), b3(mv), c_rows, True)
            a1_p, a2_p = fox.reshape(B * S, ATTN_HALF), moba.reshape(B * S, ATTN_HALF)
            hd = lambda t, n, l: t.reshape(n, l, FOX_HEADS, HEAD_DIM)
            for key, val in (("fk", hd(fk, B, S)), ("fv", hd(fv, B, S)), ("fl", logf),
                             ("mk", hd(mk, B, S)), ("mv", hd(mv, B, S))):
                outs_p[key].append(val)
            fq, fk, fv, mq, mk, mv, fl = _proj(hs, w_main, w_fl, b_fl, widths_even, True)
            logf = fl[:, :FOX_HEADS].reshape(DB, T, FOX_HEADS)
            l_new = _pad_lanes(jnp.swapaxes(logf, 1, 2), PAGE_SIZE)
            fox = _fox_decode(page_table, li, fq.reshape(DB, T, ATTN_HALF), pad_keys(fk), pad_keys(fv), l_new,
                              cfk, cfv, cfl_t)
            moba = _moba_decode(page_table, li, mq.reshape(DB, T, ATTN_HALF), pad_keys(mk), pad_keys(mv), cmk, cmv)
            a1_s, a2_s = fox.reshape(DB * T, ATTN_HALF), moba.reshape(DB * T, ATTN_HALF)
            for key, val in (("fk", hd(fk, DB, T)), ("fv", hd(fv, DB, T)), ("fl", logf),
                             ("mk", hd(mk, DB, T)), ("mv", hd(mv, DB, T))):
                outs_s[key].append(val)
            w1, w2 = w_out[:ATTN_HALF], w_out[ATTN_HALF:]
        else:
            w_main, w_small = _odd_weights(w_in_odd[li])
            w_out = w_out_odd[li].astype(BF16)
            shared = (w_main, w_small, conv_w[li], _pad_lanes(a_log[li][None, :]), _pad_lanes(dt_bias[li][None, :]),
                      gdn_norm_w[li][None, :], w_pool[li].astype(BF16), pool_scale[li][None, :])
            a1_p, a2_p, sp, sc, sg = _odd_mixer(
                hp, B, S, jnp.zeros((B, POOL_CTX, POOL_WIDTH), F32), jnp.zeros((B, CONV_CTX, 3 * GDN_WIDTH), F32),
                jnp.zeros((B, GDN_HEADS, GDN_HEAD_DIM, GDN_HEAD_DIM), F32), 0, GDN_CHUNK, *shared)
            outs_p["pool"].append(sp); outs_p["conv"].append(sc); outs_p["gdn"].append(sg)
            a1_s, a2_s, sp, sc, sg = _odd_mixer(
                hs, DB, T, state_pool[li], state_conv[li], state_gdn[li], P, T, *shared)
            outs_s["pool"].append(sp); outs_s["conv"].append(sc); outs_s["gdn"].append(sg)
            w1, w2 = w_out[:POOL_WIDTH], w_out[POOL_WIDTH:]
        wgu, wdn = w_gate_up[layer].astype(BF16), w_down[layer].astype(BF16)
        hp = _outproj_ln(a1_p, a2_p, hp, w1, w2, row(ln1_g), row(ln1_b))
        hp = _ffn_ln(hp, wgu, wdn, row(ln2_g), row(ln2_b))
        hs = _outproj_ln(a1_s, a2_s, hs, w1, w2, row(ln1_g), row(ln1_b))
        hs = _ffn_ln(hs, wgu, wdn, row(ln2_g), row(ln2_b))

    st = jnp.stack
    keys = ("fk", "fv", "fl", "mk", "mv", "pool", "conv", "gdn")
    return ((hp.reshape(B, S, D_MODEL), hs.reshape(DB, T, D_MODEL))
            + tuple(st(outs_p[k]) for k in keys) + tuple(st(outs_s[k]) for k in keys))
```

```python
import functools

import jax
import jax.numpy as jnp
from jax import lax
from jax.experimental import pallas as pl
from jax.experimental.pallas import tpu as pltpu

F32 = jnp.float32
BF16 = jnp.bfloat16
HI = lax.Precision.HIGHEST

D_MODEL = 1024
HEAD_DIM = 64
FOX_HEADS = 8
MOBA_HEADS = 8
ATTN_HALF = FOX_HEADS * HEAD_DIM
ATTN_SCALE = HEAD_DIM ** -0.5
MOBA_BLOCK = 256
MOBA_TOPK = 3
PAGE_SIZE = 128
POOL_WINDOWS = (2, 4, 8, 16)
POOL_WIDTH = 512
POOL_GROUP_WIDTH = 128
POOL_CTX = 15
GDN_HEADS = 4
GDN_HEAD_DIM = 128
GDN_WIDTH = GDN_HEADS * GDN_HEAD_DIM
GDN_SCALE = GDN_HEAD_DIM ** -0.5
GDN_CHUNK = 64
CONV_WIDTH = 4
CONV_CTX = CONV_WIDTH - 1
DEPTH = 4
DEEPNORM_ALPHA = (2 * DEPTH) ** 0.25
LN_EPS = 1e-5
NORM_EPS = 1e-6
NEG_INF = -1e30

LANES = 128
SUBLANES = 8
POOL_HALO = 16
CONV_HALO = 8
VMEM_LIMIT = 56 * 1024 * 1024


def _cparams(n_axes):
    return pltpu.CompilerParams(dimension_semantics=("arbitrary",) * n_axes,
                                vmem_limit_bytes=VMEM_LIMIT)


def _dot(a, b, precision=lax.Precision.DEFAULT):
    return jnp.dot(a, b, preferred_element_type=F32, precision=precision)


def _dot_split3(a, b):
    hi = a.astype(BF16)
    mid = (a - hi.astype(F32)).astype(BF16)
    lo = (a - hi.astype(F32) - mid.astype(F32)).astype(BF16)
    return _dot(hi, b) + _dot(mid, b) + _dot(lo, b)


def _dot_nt(a, b, precision=lax.Precision.DEFAULT):
    return lax.dot_general(a, b, (((1,), (1,)), ((), ())), preferred_element_type=F32, precision=precision)


def _split2(a):
    hi = a.astype(BF16)
    return hi, (a - hi.astype(F32)).astype(BF16)


def _mm3(a, b, dims):
    ah, al = _split2(a)
    bh, bl = _split2(b)
    dg = lambda x, y: lax.dot_general(x, y, dims, preferred_element_type=F32)
    return dg(ah, bh) + dg(ah, bl) + dg(al, bh)


def _mm(a, b):
    return _mm3(a, b, (((1,), (0,)), ((), ())))


def _mm_nt(a, b):
    return _mm3(a, b, (((1,), (1,)), ((), ())))


def _mm_tn(a, b):
    return _mm3(a, b, (((0,), (0,)), ((), ())))


def _sigmoid(x):
    return 1.0 / (1.0 + jnp.exp(-x))


def _softplus(x):
    return jnp.maximum(x, 0.0) + jnp.log1p(jnp.exp(-jnp.abs(x)))


def _layer_norm(y, g, b):
    mu = jnp.mean(y, axis=-1, keepdims=True)
    yc = y - mu
    var = jnp.mean(yc * yc, axis=-1, keepdims=True)
    return yc * lax.rsqrt(var + LN_EPS) * g + b


def _iota(shape, axis):
    return lax.broadcasted_iota(jnp.int32, shape, axis)


def _proj_kernel(x_ref, w_ref, ws_ref, bs_ref, *out_refs, widths, log_sigmoid):
    xb = x_ref[...].astype(BF16)
    off = 0
    for o_ref, wd in zip(out_refs[:-1], widths):
        o_ref[...] = _dot(xb, w_ref[:, off:off + wd])
        off += wd
    small = _dot(xb, ws_ref[...]) + bs_ref[...]
    if log_sigmoid:
        small = -_softplus(-small)
    out_refs[-1][...] = small


def _proj(x, w_main, w_small, b_small, widths, log_sigmoid):
    m = x.shape[0]
    tm = min(512, m)
    n_main = w_main.shape[1]
    outs = [jax.ShapeDtypeStruct((m, wd), F32) for wd in widths] + [jax.ShapeDtypeStruct((m, LANES), F32)]
    return pl.pallas_call(
        functools.partial(_proj_kernel, widths=widths, log_sigmoid=log_sigmoid),
        grid=(m // tm,),
        in_specs=[pl.BlockSpec((tm, D_MODEL), lambda i: (i, 0)),
                  pl.BlockSpec((D_MODEL, n_main), lambda i: (0, 0)),
                  pl.BlockSpec((D_MODEL, LANES), lambda i: (0, 0)),
                  pl.BlockSpec((1, LANES), lambda i: (0, 0))],
        out_specs=[pl.BlockSpec((tm, wd), lambda i: (i, 0)) for wd in widths]
        + [pl.BlockSpec((tm, LANES), lambda i: (i, 0))],
        out_shape=outs,
        compiler_params=_cparams(1),
        name="in_proj",
    )(x, w_main, w_small, b_small)


def _outproj_ln_kernel(a1_ref, a2_ref, x_ref, w1_ref, w2_ref, g_ref, b_ref, o_ref):
    mix = _dot(a1_ref[...].astype(BF16), w1_ref[...]) + _dot(a2_ref[...].astype(BF16), w2_ref[...])
    o_ref[...] = _layer_norm(DEEPNORM_ALPHA * x_ref[...] + mix, g_ref[...], b_ref[...])


def _outproj_ln(a1, a2, x, w1, w2, g, b):
    m = x.shape[0]
    tm = min(512, m)
    k1, k2 = a1.shape[1], a2.shape[1]
    return pl.pallas_call(
        _outproj_ln_kernel,
        grid=(m // tm,),
        in_specs=[pl.BlockSpec((tm, k1), lambda i: (i, 0)),
                  pl.BlockSpec((tm, k2), lambda i: (i, 0)),
                  pl.BlockSpec((tm, D_MODEL), lambda i: (i, 0)),
                  pl.BlockSpec((k1, D_MODEL), lambda i: (0, 0)),
                  pl.BlockSpec((k2, D_MODEL), lambda i: (0, 0)),
                  pl.BlockSpec((1, D_MODEL), lambda i: (0, 0)),
                  pl.BlockSpec((1, D_MODEL), lambda i: (0, 0))],
        out_specs=pl.BlockSpec((tm, D_MODEL), lambda i: (i, 0)),
        out_shape=jax.ShapeDtypeStruct((m, D_MODEL), F32),
        compiler_params=_cparams(1),
        name="outproj_ln",
    )(a1, a2, x, w1, w2, g, b)


def _ffn_kernel(x_ref, wg_ref, wu_ref, wd_ref, g_ref, b_ref, o_ref, xb_ref, acc_ref):
    j = pl.program_id(1)

    @pl.when(j == 0)
    def _():
        xb_ref[...] = x_ref[...].astype(BF16)
        acc_ref[...] = jnp.zeros_like(acc_ref)

    xb = xb_ref[...]
    gate = _dot(xb, wg_ref[...])
    up = _dot(xb, wu_ref[...])
    act = (gate * _sigmoid(gate) * up).astype(BF16)
    acc_ref[...] += _dot(act, wd_ref[...])

    @pl.when(j == pl.num_programs(1) - 1)
    def _():
        o_ref[...] = _layer_norm(DEEPNORM_ALPHA * x_ref[...] + acc_ref[...], g_ref[...], b_ref[...])


def _ffn_ln(x, w_gate_up, w_down, g, b):
    m = x.shape[0]
    d_ff = w_down.shape[0]
    tm = min(1024, m)
    tf = 256
    nf = d_ff // tf
    return pl.pallas_call(
        _ffn_kernel,
        grid=(m // tm, nf),
        in_specs=[pl.BlockSpec((tm, D_MODEL), lambda i, j: (i, 0)),
                  pl.BlockSpec((D_MODEL, tf), lambda i, j: (0, j)),
                  pl.BlockSpec((D_MODEL, tf), lambda i, j: (0, j + nf)),
                  pl.BlockSpec((tf, D_MODEL), lambda i, j: (j, 0)),
                  pl.BlockSpec((1, D_MODEL), lambda i, j: (0, 0)),
                  pl.BlockSpec((1, D_MODEL), lambda i, j: (0, 0))],
        out_specs=pl.BlockSpec((tm, D_MODEL), lambda i, j: (i, 0)),
        out_shape=jax.ShapeDtypeStruct((m, D_MODEL), F32),
        scratch_shapes=[pltpu.VMEM((tm, D_MODEL), BF16), pltpu.VMEM((tm, D_MODEL), F32)],
        compiler_params=_cparams(2),
        name="ffn_ln",
    )(x, w_gate_up, w_gate_up, w_down, g, b)


def _cumsum_kernel(x_ref, o_ref, *, blk):
    n = x_ref.shape[-1]
    tri = (_iota((blk, blk), 0) <= _iota((blk, blk), 1)).astype(F32)
    carry = jnp.zeros((x_ref.shape[1], 1), F32)
    for i in range(n // blk):
        y = _dot(x_ref[0, :, i * blk:(i + 1) * blk], tri, HI) + carry
        o_ref[0, :, i * blk:(i + 1) * blk] = y
        carry = y[:, blk - 1:blk]


def _cumsum_rows(x):
    b, h, s = x.shape
    return pl.pallas_call(
        functools.partial(_cumsum_kernel, blk=min(256, s)),
        grid=(b,),
        in_specs=[pl.BlockSpec((1, h, s), lambda i: (i, 0, 0))],
        out_specs=pl.BlockSpec((1, h, s), lambda i: (i, 0, 0)),
        out_shape=jax.ShapeDtypeStruct((b, h, s), F32),
        compiler_params=_cparams(1),
        name="logf_cumsum",
    )(x)


KEY_TILES_PER_UPDATE = 4


def _softmax_step(s, valid, h, m_ref, l_ref, axis):
    if valid is not None:
        s = jnp.where(valid, s, NEG_INF)
    m_old = m_ref[h]
    m_new = jnp.maximum(m_old, jnp.max(s, axis=axis, keepdims=True))
    alpha = jnp.exp(m_old - m_new)
    p = jnp.exp(s - m_new)
    if valid is not None:
        p = jnp.where(valid, p, 0.0)
    l_ref[h] = alpha * l_ref[h] + jnp.sum(p, axis=axis, keepdims=True)
    m_ref[h] = m_new
    return p, alpha


def _prompt_attn_kernel(q_ref, k_ref, v_ref, c_ref, o_ref, kb_ref, vt_ref, aux_ref, sel_ref,
                        m_ref, l_ref, acc_ref, *, t, moba):
    qi = pl.program_id(2)
    seq = k_ref.shape[1]
    nblk = seq // t
    krow = _iota((t, t), 0)
    qcol = _iota((t, t), 1)

    @pl.when(qi == 0)
    def _():
        kb_ref[...] = k_ref[0].astype(BF16)
        if moba:
            aux_ref[...] = jnp.zeros_like(aux_ref)
        for blk in range(nblk):
            rows = slice(blk * t, (blk + 1) * t)
            vt_ref[:, rows] = v_ref[0, rows, :].T.astype(BF16)
            if moba:
                aux_ref[blk:blk + 1, :] = jnp.sum(k_ref[0, rows, :], axis=0, keepdims=True) * (1.0 / MOBA_BLOCK)
            else:
                for h in range(2):
                    c_col = jnp.sum(jnp.where(krow == qcol, c_ref[0, 0, h:h + 1, rows], 0.0), axis=1, keepdims=True)
                    aux_ref[h, rows, :] = jnp.broadcast_to(c_col, (t, LANES))

    q = q_ref[0]
    head0 = _iota((t, LANES), 1) < HEAD_DIM
    q_heads = (jnp.where(head0, q, 0.0), jnp.where(head0, 0.0, q))
    qb = tuple((qh * ATTN_SCALE).astype(BF16) for qh in q_heads)

    if moba:
        nbp = aux_ref.shape[0]
        blk_id = _iota((nbp, t), 0)
        for h in range(2):
            gate = _dot_nt(aux_ref[...], q_heads[h], HI)
            gate = jnp.where(blk_id < qi, gate, NEG_INF)
            rank = jnp.zeros((nbp, t), jnp.int32)
            for j in range(nblk):
                gj = gate[j:j + 1, :]
                rank = rank + ((gj > gate) | ((gj == gate) & (blk_id > j))).astype(jnp.int32)
            sel = jnp.where((rank < MOBA_TOPK) & (blk_id < qi), 1.0, 0.0)
            for j in range(nblk):
                sel_ref[h, j] = sel[j:j + 1, :]

    m_ref[...] = jnp.full(m_ref.shape, NEG_INF, F32)
    l_ref[...] = jnp.zeros_like(l_ref)
    acc_ref[...] = jnp.zeros_like(acc_ref)

    def tiles(ki0, count, diagonal_last):
        starts = [pl.multiple_of((ki0 + i) * t, t) for i in range(count)]
        for h in range(2):
            scores, valids = [], []
            for i, k0 in enumerate(starts):
                s = _dot_nt(kb_ref[pl.ds(k0, t), :], qb[h])
                diagonal = diagonal_last and i == count - 1
                if moba:
                    valid = (krow <= qcol) if diagonal else (sel_ref[h, ki0 + i] > 0.5)
                else:
                    ck = aux_ref[h, pl.ds(k0, t), :]
                    s = s - jnp.concatenate([ck] * (t // LANES), axis=1)
                    valid = (krow <= qcol) if diagonal else None
                scores.append(s if valid is None else jnp.where(valid, s, NEG_INF))
                valids.append(valid)
            m_old = m_ref[h]
            m_new = functools.reduce(jnp.maximum, [jnp.max(s, axis=0, keepdims=True) for s in scores] + [m_old])
            alpha = jnp.exp(m_old - m_new)
            rows = slice(h * HEAD_DIM, (h + 1) * HEAD_DIM)
            l_new = alpha * l_ref[h]
            acc = acc_ref[rows, :] * alpha
            for s, valid, k0 in zip(scores, valids, starts):
                p = jnp.exp(s - m_new)
                if valid is not None:
                    p = jnp.where(valid, p, 0.0)
                l_new = l_new + jnp.sum(p, axis=0, keepdims=True)
                acc = acc + _dot(vt_ref[rows, pl.ds(k0, t)], p.astype(BF16))
            m_ref[h] = m_new
            l_ref[h] = l_new
            acc_ref[rows, :] = acc

    def body(gi, carry):
        tiles(gi * KEY_TILES_PER_UPDATE, KEY_TILES_PER_UPDATE, False)
        return carry

    full = qi // KEY_TILES_PER_UPDATE
    lax.fori_loop(0, full, body, 0)
    for rest in range(KEY_TILES_PER_UPDATE):
        @pl.when(qi - full * KEY_TILES_PER_UPDATE == rest)
        def _():
            tiles(full * KEY_TILES_PER_UPDATE, rest + 1, True)

    out = jnp.concatenate([acc_ref[0:HEAD_DIM, :] / l_ref[0], acc_ref[HEAD_DIM:, :] / l_ref[1]], axis=0)
    o_ref[0] = out.T


def _prompt_attn(q, k, v, c_rows, moba):
    b, s, _ = q.shape
    t = MOBA_BLOCK
    nblk = s // t
    n_pairs = ATTN_HALF // LANES
    nbp = -(-nblk // SUBLANES) * SUBLANES
    aux = pltpu.VMEM((nbp, LANES), F32) if moba else pltpu.VMEM((2, s, LANES), F32)
    return pl.pallas_call(
        functools.partial(_prompt_attn_kernel, t=t, moba=moba),
        grid=(b, n_pairs, nblk),
        in_specs=[pl.BlockSpec((1, t, LANES), lambda i, p, j: (i, j, p)),
                  pl.BlockSpec((1, s, LANES), lambda i, p, j: (i, 0, p)),
                  pl.BlockSpec((1, s, LANES), lambda i, p, j: (i, 0, p)),
                  pl.BlockSpec((1, 1, 2, s), lambda i, p, j: (i, p, 0, 0))],
        out_specs=pl.BlockSpec((1, t, LANES), lambda i, p, j: (i, j, p)),
        out_shape=jax.ShapeDtypeStruct((b, s, ATTN_HALF), F32),
        scratch_shapes=[pltpu.VMEM((s, LANES), BF16), pltpu.VMEM((LANES, s), BF16), aux,
                        pltpu.VMEM((2, nblk, 1, t), F32),
                        pltpu.VMEM((2, 1, t), F32), pltpu.VMEM((2, 1, t), F32),
                        pltpu.VMEM((LANES, t), F32)],
        compiler_params=_cparams(3),
        name="moba_prompt" if moba else "fox_prompt",
    )(q, k, v, c_rows)


PAGES_PER_STEP = 8
ROWS = 32
PAGE_COLS = PAGE_SIZE * FOX_HEADS


def _decay_kernel(pt_ref, ln_ref, *refs, g):
    l_refs = refs[:g]
    dn_ref, d_ref, carry_ref = refs[g:]
    j = pl.program_id(1)
    later = (_iota((PAGE_SIZE, PAGE_SIZE), 0) > _iota((PAGE_SIZE, PAGE_SIZE), 1)).astype(F32).astype(BF16)

    def page(lt):
        d8 = _dot_split3(lt, later) + carry_ref[...]
        carry_ref[...] = carry_ref[...] + jnp.sum(lt, axis=1, keepdims=True)
        return d8

    @pl.when(j == 0)
    def _():
        carry_ref[...] = jnp.zeros_like(carry_ref)
        dn_ref[0] = page(ln_ref[0])

    for i in range(g):
        d_ref[0, g - 1 - i] = page(l_refs[i][0, 0])


def _decay(page_table, li, l_new, cache_lt):
    db = l_new.shape[0]
    n_pages = page_table.shape[1]
    g = PAGES_PER_STEP
    steps = n_pages // g

    def page_map(i, n, j, pt):
        return (li, pt[n, n_pages - 1 - (j * g + i)], 0, 0)

    blk = (1, FOX_HEADS, PAGE_SIZE)
    grid_spec = pltpu.PrefetchScalarGridSpec(
        num_scalar_prefetch=1, grid=(db, steps),
        in_specs=[pl.BlockSpec(blk, lambda n, j, pt: (n, 0, 0))]
        + [pl.BlockSpec((1,) + blk, functools.partial(page_map, i)) for i in range(g)],
        out_specs=[pl.BlockSpec(blk, lambda n, j, pt: (n, 0, 0)),
                   pl.BlockSpec((1, g, FOX_HEADS, PAGE_SIZE), lambda n, j, pt: (n, steps - 1 - j, 0, 0))],
        scratch_shapes=[pltpu.VMEM((FOX_HEADS, 1), F32)])
    return pl.pallas_call(
        functools.partial(_decay_kernel, g=g),
        grid_spec=grid_spec,
        out_shape=[jax.ShapeDtypeStruct((db,) + blk[1:], F32),
                   jax.ShapeDtypeStruct((db, n_pages, FOX_HEADS, PAGE_SIZE), F32)],
        compiler_params=_cparams(2),
        name="fox_decay",
    )(page_table, l_new, *([cache_lt] * g))


def _same_head(shape):
    return (_iota(shape, 1) & (FOX_HEADS - 1)) == (_iota(shape, 0) & (FOX_HEADS - 1))


def _new_token_mask():
    shape = (ROWS, ROWS)
    return _same_head(shape) & ((_iota(shape, 1) >> 3) <= (_iota(shape, 0) >> 3))


def _flat_page(ref):
    return ref[0, 0].reshape(PAGE_COLS, HEAD_DIM).astype(BF16)


def _fox_decode_kernel(pt_ref, q_ref, kn_ref, vn_ref, dn_ref, rc_ref, d_ref, *refs, g):
    k_refs, v_refs = refs[:g], refs[g:2 * g]
    o_ref, qb_ref, m_ref, l_ref, acc_ref = refs[2 * g:]
    j = pl.program_id(1)

    def attend(s, valid, vb):
        p, alpha = _softmax_step(s, valid, 0, m_ref, l_ref, 1)
        acc_ref[...] = acc_ref[...] * alpha + _dot(p.astype(BF16), vb)

    @pl.when(j == 0)
    def _():
        qb_ref[...] = (q_ref[0] * ATTN_SCALE).astype(BF16)
        m_ref[...] = jnp.full(m_ref.shape, NEG_INF, F32)
        l_ref[...] = jnp.zeros_like(l_ref)
        acc_ref[...] = jnp.zeros_like(acc_ref)
        s = _dot_nt(qb_ref[...], kn_ref[0].astype(BF16)) + dn_ref[0] - rc_ref[0]
        attend(s, _new_token_mask(), vn_ref[0].astype(BF16))

    same_head = _same_head((ROWS, PAGE_COLS))
    scores = [jnp.where(same_head, _dot_nt(qb_ref[...], _flat_page(k_refs[i])) + d_ref[0, i] - rc_ref[0], NEG_INF)
              for i in range(g)]
    m_old = m_ref[0]
    m_new = functools.reduce(jnp.maximum, [jnp.max(s, axis=1, keepdims=True) for s in scores] + [m_old])
    alpha = jnp.exp(m_old - m_new)
    l_new = alpha * l_ref[0]
    acc = acc_ref[...] * alpha
    for i, s in enumerate(scores):
        p = jnp.exp(s - m_new)
        l_new = l_new + jnp.sum(p, axis=1, keepdims=True)
        acc = acc + _dot(p.astype(BF16), _flat_page(v_refs[i]))
    m_ref[0] = m_new
    l_ref[0] = l_new
    acc_ref[...] = acc

    @pl.when(j == pl.num_programs(1) - 1)
    def _():
        o_ref[0] = acc_ref[...] / l_ref[0]


def _page_specs(li, g):
    def page_map(i, n, j, pt):
        return (li, pt[n, j * g + i], 0, 0, 0)

    return [pl.BlockSpec((1, 1, PAGE_SIZE, FOX_HEADS, HEAD_DIM), functools.partial(page_map, i)) for i in range(g)]


def _fox_decode(page_table, li, q, k_new, v_new, d_new, row_c, d_past, cache_k, cache_v):
    db = q.shape[0]
    n_pages = page_table.shape[1]
    g = PAGES_PER_STEP
    sample = lambda n, j, pt: (n, 0, 0)
    rows = pl.BlockSpec((1, ROWS, HEAD_DIM), sample)
    in_specs = [rows, rows, rows, pl.BlockSpec((1, 1, ROWS), sample), pl.BlockSpec((1, ROWS, 1), sample),
                pl.BlockSpec((1, g, 1, PAGE_COLS), lambda n, j, pt: (n, j, 0, 0))]
    in_specs += _page_specs(li, g) * 2
    grid_spec = pltpu.PrefetchScalarGridSpec(
        num_scalar_prefetch=1, grid=(db, n_pages // g), in_specs=in_specs, out_specs=rows,
        scratch_shapes=[pltpu.VMEM((ROWS, HEAD_DIM), BF16), pltpu.VMEM((1, ROWS, 1), F32),
                        pltpu.VMEM((1, ROWS, 1), F32), pltpu.VMEM((ROWS, HEAD_DIM), F32)])
    return pl.pallas_call(
        functools.partial(_fox_decode_kernel, g=g),
        grid_spec=grid_spec,
        out_shape=jax.ShapeDtypeStruct((db, ROWS, HEAD_DIM), F32),
        compiler_params=_cparams(2),
        name="fox_decode",
    )(page_table, q, k_new, v_new, d_new, row_c, d_past, *([cache_k] * g), *([cache_v] * g))


def _moba_decode_kernel(pt_ref, q_ref, kn_ref, vn_ref, *refs, g, n_blocks):
    k_refs, v_refs = refs[:g], refs[g:2 * g]
    o_ref, qb_ref, mb_ref, lb_ref, gb_ref, ob_ref, m_ref, l_ref = refs[2 * g:]
    j = pl.program_id(1)
    lane = _iota((ROWS, LANES), 1)
    pages_per_block = MOBA_BLOCK // PAGE_SIZE
    same_head = _same_head((ROWS, PAGE_COLS))

    @pl.when(j == 0)
    def _():
        qb_ref[...] = (q_ref[0] * ATTN_SCALE).astype(BF16)
        mb_ref[...] = jnp.zeros_like(mb_ref)
        lb_ref[...] = jnp.zeros_like(lb_ref)
        gb_ref[...] = jnp.zeros_like(gb_ref)

    blocks_per_step = g // pages_per_block
    scores = [jnp.where(same_head, _dot_nt(qb_ref[...], _flat_page(kp)), NEG_INF) for kp in k_refs]
    mb, lb, gb = mb_ref[...], lb_ref[...], gb_ref[...]
    for bl in range(blocks_per_step):
        blk = j * blocks_per_step + bl
        pages = range(bl * pages_per_block, (bl + 1) * pages_per_block)
        kmean = sum(jnp.sum(k_refs[i][0, 0], axis=0) for i in pages) * (1.0 / MOBA_BLOCK)
        gate = jnp.sum(q_ref[0] * jnp.concatenate([kmean] * (ROWS // MOBA_HEADS), axis=0), axis=1, keepdims=True)
        m_b = functools.reduce(jnp.maximum, [jnp.max(scores[i], axis=1, keepdims=True) for i in pages])
        ps = [jnp.exp(scores[i] - m_b) for i in pages]
        l_b = sum(jnp.sum(p, axis=1, keepdims=True) for p in ps)
        ob_ref[blk] = sum(_dot(p.astype(BF16), _flat_page(v_refs[i])) for p, i in zip(ps, pages))
        here = lane == blk
        mb = jnp.where(here, m_b, mb)
        lb = jnp.where(here, l_b, lb)
        gb = jnp.where(here, gate, gb)
    mb_ref[...], lb_ref[...], gb_ref[...] = mb, lb, gb

    @pl.when(j == pl.num_programs(1) - 1)
    def _():
        m_ref[...] = jnp.full(m_ref.shape, NEG_INF, F32)
        l_ref[...] = jnp.zeros_like(l_ref)
        s_own = _dot_nt(qb_ref[...], kn_ref[0].astype(BF16))
        p_own, _ = _softmax_step(s_own, _new_token_mask(), 0, m_ref, l_ref, 1)
        o_own = _dot(p_own.astype(BF16), vn_ref[0].astype(BF16))
        m_own, l_own = m_ref[0], l_ref[0]
        valid = lane < n_blocks
        gate = jnp.where(valid, gb_ref[...], NEG_INF)
        rank = jnp.zeros((ROWS, LANES), jnp.int32)
        for b in range(n_blocks):
            gj = gate[:, b:b + 1]
            rank = rank + ((gj > gate) | ((gj == gate) & (lane > b))).astype(jnp.int32)
        sel = (rank < MOBA_TOPK) & valid
        m_all = jnp.maximum(jnp.max(jnp.where(sel, mb_ref[...], NEG_INF), axis=1, keepdims=True), m_own)
        wgt = jnp.where(sel, jnp.exp(jnp.where(sel, mb_ref[...] - m_all, 0.0)), 0.0)
        w_own = jnp.exp(m_own - m_all)
        den = jnp.sum(wgt * lb_ref[...], axis=1, keepdims=True) + w_own * l_own
        num = w_own * o_own
        for b in range(n_blocks):
            num = num + wgt[:, b:b + 1] * ob_ref[b]
        o_ref[0] = num / den


def _moba_decode(page_table, li, q, k_new, v_new, cache_k, cache_v):
    db = q.shape[0]
    n_pages = page_table.shape[1]
    g = PAGES_PER_STEP
    n_blocks = n_pages * PAGE_SIZE // MOBA_BLOCK
    rows = pl.BlockSpec((1, ROWS, HEAD_DIM), lambda n, j, pt: (n, 0, 0))
    grid_spec = pltpu.PrefetchScalarGridSpec(
        num_scalar_prefetch=1, grid=(db, n_pages // g),
        in_specs=[rows, rows, rows] + _page_specs(li, g) * 2, out_specs=rows,
        scratch_shapes=[pltpu.VMEM((ROWS, HEAD_DIM), BF16),
                        pltpu.VMEM((ROWS, LANES), F32), pltpu.VMEM((ROWS, LANES), F32),
                        pltpu.VMEM((ROWS, LANES), F32), pltpu.VMEM((n_blocks, ROWS, HEAD_DIM), F32),
                        pltpu.VMEM((1, ROWS, 1), F32), pltpu.VMEM((1, ROWS, 1), F32)])
    return pl.pallas_call(
        functools.partial(_moba_decode_kernel, g=g, n_blocks=n_blocks),
        grid_spec=grid_spec,
        out_shape=jax.ShapeDtypeStruct((db, ROWS, HEAD_DIM), F32),
        compiler_params=_cparams(2),
        name="moba_decode",
    )(page_table, q, k_new, v_new, *([cache_k] * g), *([cache_v] * g))


def _pool_kernel(u_ref, ctx_ref, w_ref, sc_ref, o_ref, ext_ref, *, tm, pos0):
    i = pl.program_id(1)

    @pl.when(i == 0)
    def _():
        ext_ref[0:POOL_HALO, :] = ctx_ref[0]

    @pl.when(i > 0)
    def _():
        ext_ref[0:POOL_HALO, :] = ext_ref[tm:tm + POOL_HALO, :]

    ext_ref[POOL_HALO:, :] = u_ref[0]
    pos = pos0 + i * tm + _iota((tm, 1), 0)
    for gi, win in enumerate(POOL_WINDOWS):
        cols = slice(gi * POOL_GROUP_WIDTH, (gi + 1) * POOL_GROUP_WIDTH)
        x = ext_ref[:, cols]
        tot = x
        shift = 1
        while shift < win:
            tot = tot + pltpu.roll(tot, shift, 0)
            shift *= 2
        cnt = jnp.minimum(win, pos + 1).astype(F32)
        d = tot[POOL_HALO:] / cnt - x[POOL_HALO:]
        y = _dot(d.astype(BF16), w_ref[gi])
        o_ref[0, :, cols] = y * sc_ref[:, cols]


def _pool_mix(u, ctx, w_pool, pool_scale, pos0):
    n, l, _ = u.shape
    tm = min(512, l)
    return pl.pallas_call(
        functools.partial(_pool_kernel, tm=tm, pos0=pos0),
        grid=(n, l // tm),
        in_specs=[pl.BlockSpec((1, tm, POOL_WIDTH), lambda b, i: (b, i, 0)),
                  pl.BlockSpec((1, POOL_HALO, POOL_WIDTH), lambda b, i: (b, 0, 0)),
                  pl.BlockSpec((len(POOL_WINDOWS), POOL_GROUP_WIDTH, POOL_GROUP_WIDTH), lambda b, i: (0, 0, 0)),
                  pl.BlockSpec((1, POOL_WIDTH), lambda b, i: (0, 0))],
        out_specs=pl.BlockSpec((1, tm, POOL_WIDTH), lambda b, i: (b, i, 0)),
        out_shape=jax.ShapeDtypeStruct((n, l, POOL_WIDTH), F32),
        scratch_shapes=[pltpu.VMEM((POOL_HALO + tm, POOL_WIDTH), F32)],
        compiler_params=_cparams(2),
        name="pool_mix",
    )(u, ctx, w_pool, pool_scale)


def _gdn_prep_kernel(x_ref, ctx_ref, ab_ref, cw_ref, alog_ref, dtb_ref, q_ref, k_ref, v_ref, gb_ref, ext_ref, *, tm):
    i = pl.program_id(1)

    @pl.when(i == 0)
    def _():
        ext_ref[0:CONV_HALO, :] = ctx_ref[0]

    @pl.when(i > 0)
    def _():
        ext_ref[0:CONV_HALO, :] = ext_ref[tm:tm + CONV_HALO, :]

    ext_ref[CONV_HALO:, :] = x_ref[0]
    for part, out_ref in enumerate((q_ref, k_ref, v_ref)):
        for h in range(GDN_HEADS):
            c0 = part * GDN_WIDTH + h * GDN_HEAD_DIM
            x = ext_ref[:, c0:c0 + GDN_HEAD_DIM]
            y = x[CONV_HALO:] * cw_ref[CONV_WIDTH - 1:CONV_WIDTH, c0:c0 + GDN_HEAD_DIM]
            for tap in range(CONV_WIDTH - 1):
                shifted = pltpu.roll(x, CONV_CTX - tap, 0)[CONV_HALO:]
                y = y + shifted * cw_ref[tap:tap + 1, c0:c0 + GDN_HEAD_DIM]
            y = y * _sigmoid(y)
            if part < 2:
                y = y * lax.rsqrt(jnp.sum(y * y, axis=-1, keepdims=True) + NORM_EPS)
            if part == 0:
                y = y * GDN_SCALE
            out_ref[0, :, h * GDN_HEAD_DIM:(h + 1) * GDN_HEAD_DIM] = y
    ab = ab_ref[0]
    g = -jnp.exp(alog_ref[...]) * _softplus(ab + dtb_ref[...])
    beta = _sigmoid(ab)
    gb_ref[0] = jnp.where(_iota(ab.shape, 1) < GDN_HEADS, g, beta)


def _gdn_prep(qkv, ctx, ab, conv_w, a_log_row, dt_bias_row):
    n, l, width = qkv.shape
    tm = min(256, l)
    outs = [jax.ShapeDtypeStruct((n, l, GDN_WIDTH), F32)] * 3 + [jax.ShapeDtypeStruct((n, l, LANES), F32)]
    row = lambda b, i: (b, i, 0)
    fixed = lambda b, i: (0, 0)
    return pl.pallas_call(
        functools.partial(_gdn_prep_kernel, tm=tm),
        grid=(n, l // tm),
        in_specs=[pl.BlockSpec((1, tm, width), row),
                  pl.BlockSpec((1, CONV_HALO, width), lambda b, i: (b, 0, 0)),
                  pl.BlockSpec((1, tm, LANES), row),
                  pl.BlockSpec((CONV_WIDTH, width), fixed),
                  pl.BlockSpec((1, LANES), fixed),
                  pl.BlockSpec((1, LANES), fixed)],
        out_specs=[pl.BlockSpec((1, tm, GDN_WIDTH), row)] * 3 + [pl.BlockSpec((1, tm, LANES), row)],
        out_shape=outs,
        scratch_shapes=[pltpu.VMEM((CONV_HALO + tm, width), F32)],
        compiler_params=_cparams(2),
        name="gdn_prep",
    )(qkv, ctx, ab, conv_w, a_log_row, dt_bias_row)


def _gdn_chunk_kernel(q_ref, k_ref, v_ref, gb_ref, u_ref, w_ref, qg_ref, kd_ref, at_ref, eg_ref, *, c, chunks):
    ri = _iota((c, c), 0)
    ci = _iota((c, c), 1)
    eye = ri == ci
    incl = ri >= ci
    strict = ri > ci
    eye_f = eye.astype(F32)
    levels = max(1, (c - 1).bit_length())
    pairs = [(ch, h) for ch in range(chunks) for h in range(GDN_HEADS)]
    rows = lambda ch: slice(ch * c, (ch + 1) * c)
    cols = lambda h: slice(h * GDN_HEAD_DIM, (h + 1) * GDN_HEAD_DIM)
    ks = [k_ref[0, rows(ch), cols(h)] for ch, h in pairs]
    g_cols = [gb_ref[0, rows(ch), h:h + 1] for ch, h in pairs]
    betas = [gb_ref[0, rows(ch), GDN_HEADS + h:GDN_HEADS + h + 1] for ch, h in pairs]
    gc_cols, decays = [], []
    for g_col in g_cols:
        g_row = jnp.sum(jnp.where(eye, g_col, 0.0), axis=0, keepdims=True)
        gc_col = jnp.sum(jnp.where(incl, g_row, 0.0), axis=1, keepdims=True)
        gc_row = jnp.sum(jnp.where(ri <= ci, g_col, 0.0), axis=0, keepdims=True)
        gc_cols.append(gc_col)
        decays.append(jnp.where(incl, jnp.exp(jnp.where(incl, gc_col - gc_row, 0.0)), 0.0))
    kbetas = [k * beta for k, beta in zip(ks, betas)]
    a_mats = [jnp.where(strict, _mm_nt(kb, k) * d, 0.0) for kb, k, d in zip(kbetas, ks, decays)]
    ts = [eye_f - a for a in a_mats]
    npows = [_mm(a, a) for a in a_mats]
    for lvl in range(1, levels):
        ts = [t + _mm(t, npow) for t, npow in zip(ts, npows)]
        if lvl < levels - 1:
            npows = [_mm(npow, npow) for npow in npows]
    egs = [jnp.exp(gc) for gc in gc_cols]
    for idx, (ch, h) in enumerate(pairs):
        r, cl = rows(ch), cols(h)
        q, k, gc = q_ref[0, r, cl], ks[idx], gc_cols[idx]
        u_ref[0, r, cl] = _mm(ts[idx], v_ref[0, r, cl] * betas[idx])
        w_ref[0, r, cl] = _mm(ts[idx], kbetas[idx] * egs[idx])
        qg_ref[0, r, cl] = q * egs[idx]
        kd_ref[0, r, cl] = k * jnp.exp(gc[c - 1:c, :] - gc)
        at_ref[0, r, h * c:(h + 1) * c] = jnp.where(incl, _mm_nt(q, k) * decays[idx], 0.0)
    lane = _iota((c, LANES), 1)
    for ch in range(chunks):
        eg_full = jnp.zeros((c, LANES), F32)
        for h in range(GDN_HEADS):
            eg_full = jnp.where(lane == h, egs[ch * GDN_HEADS + h], eg_full)
        eg_ref[0, rows(ch), :] = eg_full


def _gdn_chunks(q, k, v, gb, c):
    n, l, _ = q.shape
    chunks = max(1, min(4, l // c))
    tm = chunks * c
    row = lambda b, i: (b, i, 0)
    big = jax.ShapeDtypeStruct((n, l, GDN_WIDTH), F32)
    outs = [big] * 4 + [jax.ShapeDtypeStruct((n, l, GDN_HEADS * c), F32), jax.ShapeDtypeStruct((n, l, LANES), F32)]
    return pl.pallas_call(
        functools.partial(_gdn_chunk_kernel, c=c, chunks=chunks),
        grid=(n, l // tm),
        in_specs=[pl.BlockSpec((1, tm, GDN_WIDTH), row)] * 3 + [pl.BlockSpec((1, tm, LANES), row)],
        out_specs=[pl.BlockSpec((1, tm, GDN_WIDTH), row)] * 4
        + [pl.BlockSpec((1, tm, GDN_HEADS * c), row), pl.BlockSpec((1, tm, LANES), row)],
        out_shape=outs,
        compiler_params=_cparams(2),
        name="gdn_chunks",
    )(q, k, v, gb)


def _gdn_scan_kernel(u_ref, w_ref, qg_ref, kd_ref, at_ref, eg_ref, z_ref, s0_ref, nw_ref, o_ref, sout_ref, s_ref, *, c):
    i = pl.program_id(1)

    @pl.when(i == 0)
    def _():
        s_ref[...] = s0_ref[0]

    heads = range(GDN_HEADS)
    cols = lambda h: slice(h * GDN_HEAD_DIM, (h + 1) * GDN_HEAD_DIM)
    states = [s_ref[h] for h in heads]
    v_news = [u_ref[0, :, cols(h)] - _mm(w_ref[0, :, cols(h)], states[h]) for h in heads]
    outs = [_mm(qg_ref[0, :, cols(h)], states[h]) + _mm(at_ref[0, :, h * c:(h + 1) * c], v_news[h]) for h in heads]
    for h in heads:
        s_ref[h] = states[h] * eg_ref[0, c - 1:c, h:h + 1] + _mm_tn(kd_ref[0, :, cols(h)], v_news[h])
        o = outs[h]
        o = o * lax.rsqrt(jnp.mean(o * o, axis=-1, keepdims=True) + NORM_EPS) * nw_ref[...]
        z = z_ref[0, :, cols(h)]
        o_ref[0, :, cols(h)] = o * (z * _sigmoid(z))

    @pl.when(i == pl.num_programs(1) - 1)
    def _():
        sout_ref[0] = s_ref[...]


def _gdn_scan(u, w, qg, kd, at, eg, z, s0, norm_w, c):
    n, l, _ = u.shape
    row = lambda b, i: (b, i, 0)
    state = lambda b, i: (b, 0, 0, 0)
    st_shape = (1, GDN_HEADS, GDN_HEAD_DIM, GDN_HEAD_DIM)
    return pl.pallas_call(
        functools.partial(_gdn_scan_kernel, c=c),
        grid=(n, l // c),
        in_specs=[pl.BlockSpec((1, c, GDN_WIDTH), row)] * 4
        + [pl.BlockSpec((1, c, GDN_HEADS * c), row), pl.BlockSpec((1, c, LANES), row),
           pl.BlockSpec((1, c, GDN_WIDTH), row), pl.BlockSpec(st_shape, state),
           pl.BlockSpec((1, GDN_HEAD_DIM), lambda b, i: (0, 0))],
        out_specs=[pl.BlockSpec((1, c, GDN_WIDTH), row), pl.BlockSpec(st_shape, state)],
        out_shape=[jax.ShapeDtypeStruct((n, l, GDN_WIDTH), F32),
                   jax.ShapeDtypeStruct((n,) + st_shape[1:], F32)],
        scratch_shapes=[pltpu.VMEM(st_shape[1:], F32)],
        compiler_params=_cparams(2),
        name="gdn_scan",
    )(u, w, qg, kd, at, eg, z, s0, norm_w)


def _pad_lanes(x, width=LANES):
    return jnp.pad(x, [(0, 0)] * (x.ndim - 1) + [(0, width - x.shape[-1])])


def _even_weights(w_in, b_f):
    fq, fk, fv, fl, mq, mk, mv = jnp.split(
        w_in, [ATTN_HALF, 2 * ATTN_HALF, 3 * ATTN_HALF, 3 * ATTN_HALF + FOX_HEADS,
               4 * ATTN_HALF + FOX_HEADS, 5 * ATTN_HALF + FOX_HEADS], axis=1)
    w_main = jnp.concatenate([fq, fk, fv, mq, mk, mv], axis=1).astype(BF16)
    return w_main, _pad_lanes(fl).astype(BF16), _pad_lanes(b_f[None, :])


def _odd_weights(w_in):
    u, qkv, a, b, z = jnp.split(
        w_in, [POOL_WIDTH, POOL_WIDTH + 3 * GDN_WIDTH, POOL_WIDTH + 3 * GDN_WIDTH + GDN_HEADS,
               POOL_WIDTH + 3 * GDN_WIDTH + 2 * GDN_HEADS], axis=1)
    w_main = jnp.concatenate([u, qkv, z], axis=1).astype(BF16)
    return w_main, _pad_lanes(jnp.concatenate([a, b], axis=1)).astype(BF16)


def _odd_mixer(h, n, l, pool_ctx, conv_ctx, s0, pos0, chunk, w_main, w_small, conv_w, a_log_row, dt_bias_row,
               norm_w, w_pool, pool_scale):
    u, qkv, z, ab = _proj(h, w_main, w_small, jnp.zeros((1, LANES), F32),
                          (POOL_WIDTH, 3 * GDN_WIDTH, GDN_WIDTH), False)
    u3, qkv3, z3, ab3 = (t.reshape(n, l, -1) for t in (u, qkv, z, ab))
    lp = -(-l // SUBLANES) * SUBLANES
    padr = lambda t: jnp.pad(t, ((0, 0), (0, lp - l), (0, 0)))
    ctx_p = jnp.pad(pool_ctx, ((0, 0), (POOL_HALO - POOL_CTX, 0), (0, 0)))
    ctx_c = jnp.pad(conv_ctx, ((0, 0), (CONV_HALO - CONV_CTX, 0), (0, 0)))
    y_pool = _pool_mix(padr(u3), ctx_p, w_pool, pool_scale, pos0)[:, :l]
    q, k, v, gb = _gdn_prep(padr(qkv3), ctx_c, padr(ab3), conv_w, a_log_row, dt_bias_row)
    if lp != l:
        gb = gb * (jnp.arange(lp) < l).astype(F32)[None, :, None]
    if chunk % SUBLANES:
        assert chunk == l
        chunk = lp
    c = chunk
    uu, ww, qg, kd, at, eg = _gdn_chunks(q, k, v, gb, c)
    o, s_new = _gdn_scan(uu, ww, qg, kd, at, eg, padr(z3), s0, norm_w, c)
    new_pool = jnp.concatenate([pool_ctx, u3], axis=1)[:, -POOL_CTX:]
    new_conv = jnp.concatenate([conv_ctx, qkv3], axis=1)[:, -CONV_CTX:]
    return (y_pool.reshape(n * l, POOL_WIDTH), o[:, :l].reshape(n * l, GDN_WIDTH), new_pool, new_conv, s_new)


def kernel(x_prompt, x_sample, cache_fox_k, cache_fox_v, cache_fox_logf, cache_moba_k, cache_moba_v, state_pool, state_conv, state_gdn, page_table, w_in_even, b_forget, w_out_even, w_in_odd, conv_w, a_log, dt_bias, gdn_norm_w, w_pool, pool_scale, w_out_odd, ln1_g, ln1_b, ln2_g, ln2_b, w_gate_up, w_down):
    B, S, _ = x_prompt.shape
    DB, T, _ = x_sample.shape
    P = page_table.shape[1] * PAGE_SIZE
    hp = x_prompt.reshape(B * S, D_MODEL)
    hs = x_sample.reshape(DB * T, D_MODEL)

    assert T * FOX_HEADS == ROWS and P % MOBA_BLOCK == 0
    cfl_t = jnp.swapaxes(cache_fox_logf, 2, 3)

    outs_p = {k: [] for k in ("fk", "fv", "fl", "mk", "mv", "pool", "conv", "gdn")}
    outs_s = {k: [] for k in ("fk", "fv", "fl", "mk", "mv", "pool", "conv", "gdn")}
    widths_even = (ATTN_HALF,) * 6
    for layer in range(DEPTH):
        li = layer // 2
        row = lambda a: a[layer][None, :]
        if layer % 2 == 0:
            w_main, w_fl, b_fl = _even_weights(w_in_even[li], b_forget[li])
            w_out = w_out_even[li].astype(BF16)
            fq, fk, fv, mq, mk, mv, fl = _proj(hp, w_main, w_fl, b_fl, widths_even, True)
            b3 = lambda t: t.reshape(B, S, ATTN_HALF)
            logf = fl[:, :FOX_HEADS].reshape(B, S, FOX_HEADS)
            c_rows = _cumsum_rows(jnp.swapaxes(logf, 1, 2)).reshape(B, FOX_HEADS // 2, 2, S)
            fox = _prompt_attn(b3(fq), b3(fk), b3(fv), c_rows, False)
            moba = _prompt_attn(b3(mq), b3(mk), b3(mv), c_rows, True)
            a1_p, a2_p = fox.reshape(B * S, ATTN_HALF), moba.reshape(B * S, ATTN_HALF)
            hd = lambda t, n, l: t.reshape(n, l, FOX_HEADS, HEAD_DIM)
            for key, val in (("fk", hd(fk, B, S)), ("fv", hd(fv, B, S)), ("fl", logf),
                             ("mk", hd(mk, B, S)), ("mv", hd(mv, B, S))):
                outs_p[key].append(val)
            fq, fk, fv, mq, mk, mv, fl = _proj(hs, w_main, w_fl, b_fl, widths_even, True)
            logf = fl[:, :FOX_HEADS].reshape(DB, T, FOX_HEADS)
            l_new = _pad_lanes(jnp.swapaxes(logf, 1, 2), PAGE_SIZE)
            d_new8, d_past8 = _decay(page_table, li, l_new, cfl_t)
            d_new = jnp.swapaxes(d_new8[:, :, :T], 1, 2).reshape(DB, 1, ROWS)
            d_past = jnp.swapaxes(d_past8, 2, 3).reshape(DB, -1, 1, PAGE_COLS)
            rows = lambda t: t.reshape(DB, ROWS, HEAD_DIM)
            fox = _fox_decode(page_table, li, rows(fq), rows(fk), rows(fv), d_new, d_new.reshape(DB, ROWS, 1), d_past,
                              cache_fox_k, cache_fox_v)
            moba = _moba_decode(page_table, li, rows(mq), rows(mk), rows(mv), cache_moba_k, cache_moba_v)
            a1_s, a2_s = fox.reshape(DB * T, ATTN_HALF), moba.reshape(DB * T, ATTN_HALF)
            for key, val in (("fk", hd(fk, DB, T)), ("fv", hd(fv, DB, T)), ("fl", logf),
                             ("mk", hd(mk, DB, T)), ("mv", hd(mv, DB, T))):
                outs_s[key].append(val)
            w1, w2 = w_out[:ATTN_HALF], w_out[ATTN_HALF:]
        else:
            w_main, w_small = _odd_weights(w_in_odd[li])
            w_out = w_out_odd[li].astype(BF16)
            shared = (w_main, w_small, conv_w[li], _pad_lanes(a_log[li][None, :]), _pad_lanes(dt_bias[li][None, :]),
                      gdn_norm_w[li][None, :], w_pool[li].astype(BF16), pool_scale[li][None, :])
            a1_p, a2_p, sp, sc, sg = _odd_mixer(
                hp, B, S, jnp.zeros((B, POOL_CTX, POOL_WIDTH), F32), jnp.zeros((B, CONV_CTX, 3 * GDN_WIDTH), F32),
                jnp.zeros((B, GDN_HEADS, GDN_HEAD_DIM, GDN_HEAD_DIM), F32), 0, GDN_CHUNK, *shared)
            outs_p["pool"].append(sp); outs_p["conv"].append(sc); outs_p["gdn"].append(sg)
            a1_s, a2_s, sp, sc, sg = _odd_mixer(
                hs, DB, T, state_pool[li], state_conv[li], state_gdn[li], P, T, *shared)
            outs_s["pool"].append(sp); outs_s["conv"].append(sc); outs_s["gdn"].append(sg)
            w1, w2 = w_out[:POOL_WIDTH], w_out[POOL_WIDTH:]
        wgu, wdn = w_gate_up[layer].astype(BF16), w_down[layer].astype(BF16)
        hp = _outproj_ln(a1_p, a2_p, hp, w1, w2, row(ln1_g), row(ln1_b))
        hp = _ffn_ln(hp, wgu, wdn, row(ln2_g), row(ln2_b))
        hs = _outproj_ln(a1_s, a2_s, hs, w1, w2, row(ln1_g), row(ln1_b))
        hs = _ffn_ln(hs, wgu, wdn, row(ln2_g), row(ln2_b))

    st = jnp.stack
    keys = ("fk", "fv", "fl", "mk", "mv", "pool", "conv", "gdn")
    return ((hp.reshape(B, S, D_MODEL), hs.reshape(DB, T, D_MODEL))
            + tuple(st(outs_p[k]) for k in keys) + tuple(st(outs_s[k]) for k in keys))
```

```python
import functools

import jax
import jax.numpy as jnp
from jax import lax
from jax.experimental import pallas as pl
from jax.experimental.pallas import tpu as pltpu

F32 = jnp.float32
BF16 = jnp.bfloat16
HI = lax.Precision.HIGHEST

D_MODEL = 1024
HEAD_DIM = 64
FOX_HEADS = 8
MOBA_HEADS = 8
ATTN_HALF = FOX_HEADS * HEAD_DIM
ATTN_SCALE = HEAD_DIM ** -0.5
MOBA_BLOCK = 256
MOBA_TOPK = 3
PAGE_SIZE = 128
POOL_WINDOWS = (2, 4, 8, 16)
POOL_WIDTH = 512
POOL_GROUP_WIDTH = 128
POOL_CTX = 15
GDN_HEADS = 4
GDN_HEAD_DIM = 128
GDN_WIDTH = GDN_HEADS * GDN_HEAD_DIM
GDN_SCALE = GDN_HEAD_DIM ** -0.5
GDN_CHUNK = 64
CONV_WIDTH = 4
CONV_CTX = CONV_WIDTH - 1
DEPTH = 4
DEEPNORM_ALPHA = (2 * DEPTH) ** 0.25
LN_EPS = 1e-5
NORM_EPS = 1e-6
NEG_INF = -1e30

LANES = 128
SUBLANES = 8
POOL_HALO = 16
CONV_HALO = 8
VMEM_LIMIT = 56 * 1024 * 1024


def _cparams(n_axes):
    return pltpu.CompilerParams(dimension_semantics=("arbitrary",) * n_axes,
                                vmem_limit_bytes=VMEM_LIMIT)


def _dot(a, b, precision=lax.Precision.DEFAULT):
    return jnp.dot(a, b, preferred_element_type=F32, precision=precision)


def _dot_split3(a, b):
    hi = a.astype(BF16)
    mid = (a - hi.astype(F32)).astype(BF16)
    lo = (a - hi.astype(F32) - mid.astype(F32)).astype(BF16)
    return _dot(hi, b) + _dot(mid, b) + _dot(lo, b)


def _dot_nt(a, b, precision=lax.Precision.DEFAULT):
    return lax.dot_general(a, b, (((1,), (1,)), ((), ())), preferred_element_type=F32, precision=precision)


def _split2(a):
    hi = a.astype(BF16)
    return hi, (a - hi.astype(F32)).astype(BF16)


def _mm3(a, b, dims):
    ah, al = _split2(a)
    bh, bl = _split2(b)
    dg = lambda x, y: lax.dot_general(x, y, dims, preferred_element_type=F32)
    return dg(ah, bh) + dg(ah, bl) + dg(al, bh)


def _mm(a, b):
    return _mm3(a, b, (((1,), (0,)), ((), ())))


def _mm_nt(a, b):
    return _mm3(a, b, (((1,), (1,)), ((), ())))


def _mm_tn(a, b):
    return _mm3(a, b, (((0,), (0,)), ((), ())))


def _sigmoid(x):
    return 1.0 / (1.0 + jnp.exp(-x))


def _softplus(x):
    return jnp.maximum(x, 0.0) + jnp.log1p(jnp.exp(-jnp.abs(x)))


def _layer_norm(y, g, b):
    mu = jnp.mean(y, axis=-1, keepdims=True)
    yc = y - mu
    var = jnp.mean(yc * yc, axis=-1, keepdims=True)
    return yc * lax.rsqrt(var + LN_EPS) * g + b


def _iota(shape, axis):
    return lax.broadcasted_iota(jnp.int32, shape, axis)


def _proj_kernel(x_ref, w_ref, ws_ref, bs_ref, *out_refs, widths, log_sigmoid):
    xb = x_ref[...].astype(BF16)
    off = 0
    for o_ref, wd in zip(out_refs[:-1], widths):
        o_ref[...] = _dot(xb, w_ref[:, off:off + wd])
        off += wd
    small = _dot(xb, ws_ref[...]) + bs_ref[...]
    if log_sigmoid:
        small = -_softplus(-small)
    out_refs[-1][...] = small


def _proj(x, w_main, w_small, b_small, widths, log_sigmoid):
    m = x.shape[0]
    tm = min(512, m)
    n_main = w_main.shape[1]
    outs = [jax.ShapeDtypeStruct((m, wd), F32) for wd in widths] + [jax.ShapeDtypeStruct((m, LANES), F32)]
    return pl.pallas_call(
        functools.partial(_proj_kernel, widths=widths, log_sigmoid=log_sigmoid),
        grid=(m // tm,),
        in_specs=[pl.BlockSpec((tm, D_MODEL), lambda i: (i, 0)),
                  pl.BlockSpec((D_MODEL, n_main), lambda i: (0, 0)),
                  pl.BlockSpec((D_MODEL, LANES), lambda i: (0, 0)),
                  pl.BlockSpec((1, LANES), lambda i: (0, 0))],
        out_specs=[pl.BlockSpec((tm, wd), lambda i: (i, 0)) for wd in widths]
        + [pl.BlockSpec((tm, LANES), lambda i: (i, 0))],
        out_shape=outs,
        compiler_params=_cparams(1),
        name="in_proj",
    )(x, w_main, w_small, b_small)


def _outproj_ln_kernel(a1_ref, a2_ref, x_ref, w1_ref, w2_ref, g_ref, b_ref, o_ref):
    mix = _dot(a1_ref[...].astype(BF16), w1_ref[...]) + _dot(a2_ref[...].astype(BF16), w2_ref[...])
    o_ref[...] = _layer_norm(DEEPNORM_ALPHA * x_ref[...] + mix, g_ref[...], b_ref[...])


def _outproj_ln(a1, a2, x, w1, w2, g, b):
    m = x.shape[0]
    tm = min(512, m)
    k1, k2 = a1.shape[1], a2.shape[1]
    return pl.pallas_call(
        _outproj_ln_kernel,
        grid=(m // tm,),
        in_specs=[pl.BlockSpec((tm, k1), lambda i: (i, 0)),
                  pl.BlockSpec((tm, k2), lambda i: (i, 0)),
                  pl.BlockSpec((tm, D_MODEL), lambda i: (i, 0)),
                  pl.BlockSpec((k1, D_MODEL), lambda i: (0, 0)),
                  pl.BlockSpec((k2, D_MODEL), lambda i: (0, 0)),
                  pl.BlockSpec((1, D_MODEL), lambda i: (0, 0)),
                  pl.BlockSpec((1, D_MODEL), lambda i: (0, 0))],
        out_specs=pl.BlockSpec((tm, D_MODEL), lambda i: (i, 0)),
        out_shape=jax.ShapeDtypeStruct((m, D_MODEL), F32),
        compiler_params=_cparams(1),
        name="outproj_ln",
    )(a1, a2, x, w1, w2, g, b)


def _ffn_kernel(x_ref, wg_ref, wu_ref, wd_ref, g_ref, b_ref, o_ref, xb_ref, acc_ref):
    j = pl.program_id(1)

    @pl.when(j == 0)
    def _():
        xb_ref[...] = x_ref[...].astype(BF16)
        acc_ref[...] = jnp.zeros_like(acc_ref)

    xb = xb_ref[...]
    gate = _dot(xb, wg_ref[...])
    up = _dot(xb, wu_ref[...])
    act = (gate * _sigmoid(gate) * up).astype(BF16)
    acc_ref[...] += _dot(act, wd_ref[...])

    @pl.when(j == pl.num_programs(1) - 1)
    def _():
        o_ref[...] = _layer_norm(DEEPNORM_ALPHA * x_ref[...] + acc_ref[...], g_ref[...], b_ref[...])


def _ffn_ln(x, w_gate_up, w_down, g, b):
    m = x.shape[0]
    d_ff = w_down.shape[0]
    tm = min(1024, m)
    tf = 256
    nf = d_ff // tf
    return pl.pallas_call(
        _ffn_kernel,
        grid=(m // tm, nf),
        in_specs=[pl.BlockSpec((tm, D_MODEL), lambda i, j: (i, 0)),
                  pl.BlockSpec((D_MODEL, tf), lambda i, j: (0, j)),
                  pl.BlockSpec((D_MODEL, tf), lambda i, j: (0, j + nf)),
                  pl.BlockSpec((tf, D_MODEL), lambda i, j: (j, 0)),
                  pl.BlockSpec((1, D_MODEL), lambda i, j: (0, 0)),
                  pl.BlockSpec((1, D_MODEL), lambda i, j: (0, 0))],
        out_specs=pl.BlockSpec((tm, D_MODEL), lambda i, j: (i, 0)),
        out_shape=jax.ShapeDtypeStruct((m, D_MODEL), F32),
        scratch_shapes=[pltpu.VMEM((tm, D_MODEL), BF16), pltpu.VMEM((tm, D_MODEL), F32)],
        compiler_params=_cparams(2),
        name="ffn_ln",
    )(x, w_gate_up, w_gate_up, w_down, g, b)


def _cumsum_kernel(x_ref, o_ref, *, blk):
    n = x_ref.shape[-1]
    tri = (_iota((blk, blk), 0) <= _iota((blk, blk), 1)).astype(F32)
    carry = jnp.zeros((x_ref.shape[1], 1), F32)
    for i in range(n // blk):
        y = _dot(x_ref[0, :, i * blk:(i + 1) * blk], tri, HI) + carry
        o_ref[0, :, i * blk:(i + 1) * blk] = y
        carry = y[:, blk - 1:blk]


def _cumsum_rows(x):
    b, h, s = x.shape
    return pl.pallas_call(
        functools.partial(_cumsum_kernel, blk=min(256, s)),
        grid=(b,),
        in_specs=[pl.BlockSpec((1, h, s), lambda i: (i, 0, 0))],
        out_specs=pl.BlockSpec((1, h, s), lambda i: (i, 0, 0)),
        out_shape=jax.ShapeDtypeStruct((b, h, s), F32),
        compiler_params=_cparams(1),
        name="logf_cumsum",
    )(x)


KEY_TILES_PER_UPDATE = 4
LOG2E = 1.4426950408889634


def _softmax_step(s, valid, h, m_ref, l_ref, axis):
    if valid is not None:
        s = jnp.where(valid, s, NEG_INF)
    m_old = m_ref[h]
    m_new = jnp.maximum(m_old, jnp.max(s, axis=axis, keepdims=True))
    alpha = jnp.exp(m_old - m_new)
    p = jnp.exp(s - m_new)
    if valid is not None:
        p = jnp.where(valid, p, 0.0)
    l_ref[h] = alpha * l_ref[h] + jnp.sum(p, axis=axis, keepdims=True)
    m_ref[h] = m_new
    return p, alpha


def _prompt_attn_kernel(q_ref, k_ref, v_ref, c_ref, o_ref, kb_ref, vt_ref, aux_ref, sel_ref,
                        m_ref, l_ref, acc_ref, *, t, moba):
    qi = pl.program_id(2)
    seq = k_ref.shape[1]
    nblk = seq // t
    krow = _iota((t, t), 0)
    qcol = _iota((t, t), 1)

    @pl.when(qi == 0)
    def _():
        kb_ref[...] = k_ref[0].astype(BF16)
        if moba:
            aux_ref[...] = jnp.zeros_like(aux_ref)
        for blk in range(nblk):
            rows = slice(blk * t, (blk + 1) * t)
            vt_ref[:, rows] = v_ref[0, rows, :].T.astype(BF16)
            if moba:
                aux_ref[blk:blk + 1, :] = jnp.sum(k_ref[0, rows, :], axis=0, keepdims=True) * (1.0 / MOBA_BLOCK)
            else:
                for h in range(2):
                    c_col = jnp.sum(jnp.where(krow == qcol, c_ref[0, 0, h:h + 1, rows], 0.0), axis=1, keepdims=True)
                    aux_ref[h, rows, :] = jnp.broadcast_to(c_col * LOG2E, (t, LANES))

    q = q_ref[0]
    head0 = _iota((t, LANES), 1) < HEAD_DIM
    q_heads = (jnp.where(head0, q, 0.0), jnp.where(head0, 0.0, q))
    qb = tuple((qh * (ATTN_SCALE * LOG2E)).astype(BF16) for qh in q_heads)

    if moba:
        nbp = aux_ref.shape[0]
        blk_id = _iota((nbp, t), 0)
        for h in range(2):
            gate = _dot_nt(aux_ref[...], q_heads[h], HI)
            gate = jnp.where(blk_id < qi, gate, NEG_INF)
            rank = jnp.zeros((nbp, t), jnp.int32)
            for j in range(nblk):
                gj = gate[j:j + 1, :]
                rank = rank + ((gj > gate) | ((gj == gate) & (blk_id > j))).astype(jnp.int32)
            sel = jnp.where((rank < MOBA_TOPK) & (blk_id < qi), 1.0, 0.0)
            for j in range(nblk):
                sel_ref[h, j] = sel[j:j + 1, :]

    m_ref[...] = jnp.full(m_ref.shape, NEG_INF, F32)
    l_ref[...] = jnp.zeros_like(l_ref)
    acc_ref[...] = jnp.zeros_like(acc_ref)

    def tiles(ki0, count, diagonal_last):
        starts = [pl.multiple_of((ki0 + i) * t, t) for i in range(count)]
        for h in range(2):
            scores, valids = [], []
            for i, k0 in enumerate(starts):
                s = _dot_nt(kb_ref[pl.ds(k0, t), :], qb[h])
                diagonal = diagonal_last and i == count - 1
                if moba:
                    valid = (krow <= qcol) if diagonal else (sel_ref[h, ki0 + i] > 0.5)
                else:
                    ck = aux_ref[h, pl.ds(k0, t), :]
                    s = s - jnp.concatenate([ck] * (t // LANES), axis=1)
                    valid = (krow <= qcol) if diagonal else None
                scores.append(s if valid is None else jnp.where(valid, s, NEG_INF))
                valids.append(valid)
            m_old = m_ref[h]
            m_new = functools.reduce(jnp.maximum, [jnp.max(s, axis=0, keepdims=True) for s in scores] + [m_old])
            alpha = jnp.exp2(m_old - m_new)
            rows = slice(h * HEAD_DIM, (h + 1) * HEAD_DIM)
            l_new = alpha * l_ref[h]
            acc = acc_ref[rows, :] * alpha
            for s, valid, k0 in zip(scores, valids, starts):
                p = jnp.exp2(s - m_new)
                if valid is not None:
                    p = jnp.where(valid, p, 0.0)
                l_new = l_new + jnp.sum(p, axis=0, keepdims=True)
                acc = acc + _dot(vt_ref[rows, pl.ds(k0, t)], p.astype(BF16))
            m_ref[h] = m_new
            l_ref[h] = l_new
            acc_ref[rows, :] = acc

    def body(gi, carry):
        tiles(gi * KEY_TILES_PER_UPDATE, KEY_TILES_PER_UPDATE, False)
        return carry

    full = qi // KEY_TILES_PER_UPDATE
    lax.fori_loop(0, full, body, 0)
    for rest in range(KEY_TILES_PER_UPDATE):
        @pl.when(qi - full * KEY_TILES_PER_UPDATE == rest)
        def _():
            tiles(full * KEY_TILES_PER_UPDATE, rest + 1, True)

    out = jnp.concatenate([acc_ref[0:HEAD_DIM, :] / l_ref[0], acc_ref[HEAD_DIM:, :] / l_ref[1]], axis=0)
    o_ref[0] = out.T


def _prompt_attn(q, k, v, c_rows, moba):
    b, s, _ = q.shape
    t = MOBA_BLOCK
    nblk = s // t
    n_pairs = ATTN_HALF // LANES
    nbp = -(-nblk // SUBLANES) * SUBLANES
    aux = pltpu.VMEM((nbp, LANES), F32) if moba else pltpu.VMEM((2, s, LANES), F32)
    return pl.pallas_call(
        functools.partial(_prompt_attn_kernel, t=t, moba=moba),
        grid=(b, n_pairs, nblk),
        in_specs=[pl.BlockSpec((1, t, LANES), lambda i, p, j: (i, j, p)),
                  pl.BlockSpec((1, s, LANES), lambda i, p, j: (i, 0, p)),
                  pl.BlockSpec((1, s, LANES), lambda i, p, j: (i, 0, p)),
                  pl.BlockSpec((1, 1, 2, s), lambda i, p, j: (i, p, 0, 0))],
        out_specs=pl.BlockSpec((1, t, LANES), lambda i, p, j: (i, j, p)),
        out_shape=jax.ShapeDtypeStruct((b, s, ATTN_HALF), F32),
        scratch_shapes=[pltpu.VMEM((s, LANES), BF16), pltpu.VMEM((LANES, s), BF16), aux,
                        pltpu.VMEM((2, nblk, 1, t), F32),
                        pltpu.VMEM((2, 1, t), F32), pltpu.VMEM((2, 1, t), F32),
                        pltpu.VMEM((LANES, t), F32)],
        compiler_params=_cparams(3),
        name="moba_prompt" if moba else "fox_prompt",
    )(q, k, v, c_rows)


PAGES_PER_STEP = 8
ROWS = 32


def _decay_kernel(pt_ref, ln_ref, *refs, g):
    l_refs = refs[:g]
    dn_ref, d_ref, carry_ref = refs[g:]
    j = pl.program_id(1)
    later = (_iota((PAGE_SIZE, PAGE_SIZE), 0) > _iota((PAGE_SIZE, PAGE_SIZE), 1)).astype(F32).astype(BF16)

    def page(lt):
        d8 = _dot_split3(lt, later) + carry_ref[...]
        carry_ref[...] = carry_ref[...] + jnp.sum(lt, axis=1, keepdims=True)
        return d8

    @pl.when(j == 0)
    def _():
        carry_ref[...] = jnp.zeros_like(carry_ref)
        dn_ref[0] = page(ln_ref[0])

    for i in range(g):
        d_ref[0, g - 1 - i] = page(l_refs[i][0, 0])


def _decay(page_table, li, l_new, cache_lt):
    db = l_new.shape[0]
    n_pages = page_table.shape[1]
    g = PAGES_PER_STEP
    steps = n_pages // g

    def page_map(i, n, j, pt):
        return (li, pt[n, n_pages - 1 - (j * g + i)], 0, 0)

    blk = (1, FOX_HEADS, PAGE_SIZE)
    grid_spec = pltpu.PrefetchScalarGridSpec(
        num_scalar_prefetch=1, grid=(db, steps),
        in_specs=[pl.BlockSpec(blk, lambda n, j, pt: (n, 0, 0))]
        + [pl.BlockSpec((1,) + blk, functools.partial(page_map, i)) for i in range(g)],
        out_specs=[pl.BlockSpec(blk, lambda n, j, pt: (n, 0, 0)),
                   pl.BlockSpec((1, g, FOX_HEADS, PAGE_SIZE), lambda n, j, pt: (n, steps - 1 - j, 0, 0))],
        scratch_shapes=[pltpu.VMEM((FOX_HEADS, 1), F32)])
    return pl.pallas_call(
        functools.partial(_decay_kernel, g=g),
        grid_spec=grid_spec,
        out_shape=[jax.ShapeDtypeStruct((db,) + blk[1:], F32),
                   jax.ShapeDtypeStruct((db, n_pages, FOX_HEADS, PAGE_SIZE), F32)],
        compiler_params=_cparams(2),
        name="fox_decay",
    )(page_table, l_new, *([cache_lt] * g))


def _block_diag_queries(q):
    t = q.shape[0]
    rep = jnp.concatenate([jnp.broadcast_to(q[i:i + 1], (FOX_HEADS, ATTN_HALF)) for i in range(t)], axis=0)
    keep = (_iota(rep.shape, 1) // HEAD_DIM) == (_iota(rep.shape, 0) % FOX_HEADS)
    return jnp.where(keep, rep, 0.0)


def _heads_to_tokens(o):
    keep = (_iota(o.shape, 1) // HEAD_DIM) == (_iota(o.shape, 0) % FOX_HEADS)
    o = jnp.where(keep, o, 0.0)
    t = o.shape[0] // FOX_HEADS
    return jnp.sum(o.reshape(t, FOX_HEADS, ATTN_HALF), axis=1)


def _page_bf16(ref):
    return ref[0, 0].astype(BF16)


def _fox_decode_kernel(pt_ref, q_ref, kn_ref, vn_ref, dn_ref, d_ref, *refs, g):
    k_refs, v_refs = refs[:g], refs[g:2 * g]
    o_ref, qb_ref, rc_ref, m_ref, l_ref, acc_ref = refs[2 * g:]
    j = pl.program_id(1)
    lane = _iota((ROWS, PAGE_SIZE), 1)
    tok = _iota((ROWS, PAGE_SIZE), 0) // FOX_HEADS
    per_row = lambda d8: jnp.concatenate([d8] * (ROWS // FOX_HEADS), axis=0)

    @pl.when(j == 0)
    def _():
        qb_ref[...] = (_block_diag_queries(q_ref[0]) * ATTN_SCALE).astype(BF16)
        d = per_row(dn_ref[0])
        rc_ref[...] = jnp.sum(jnp.where(lane == tok, d, 0.0), axis=1, keepdims=True)
        m_ref[...] = jnp.full(m_ref.shape, NEG_INF, F32)
        l_ref[...] = jnp.zeros_like(l_ref)
        s = _dot(qb_ref[...], kn_ref[0].astype(BF16)) + d - rc_ref[...]
        p, _ = _softmax_step(s, lane <= tok, 0, m_ref, l_ref, 1)
        acc_ref[...] = _dot_nt(p.astype(BF16), vn_ref[0].astype(BF16))

    scores = [_dot(qb_ref[...], _page_bf16(k_refs[i])) + per_row(d_ref[0, i]) - rc_ref[...] for i in range(g)]
    m_old = m_ref[0]
    m_new = functools.reduce(jnp.maximum, [jnp.max(s, axis=1, keepdims=True) for s in scores] + [m_old])
    alpha = jnp.exp(m_old - m_new)
    l_new = alpha * l_ref[0]
    acc = acc_ref[...] * alpha
    for i, s in enumerate(scores):
        p = jnp.exp(s - m_new)
        l_new = l_new + jnp.sum(p, axis=1, keepdims=True)
        acc = acc + _dot_nt(p.astype(BF16), _page_bf16(v_refs[i]))
    m_ref[0] = m_new
    l_ref[0] = l_new
    acc_ref[...] = acc

    @pl.when(j == pl.num_programs(1) - 1)
    def _():
        o_ref[0] = _heads_to_tokens(acc_ref[...] / l_ref[0])


def _page_specs(li, g):
    def page_map(i, n, j, pt):
        return (li, pt[n, j * g + i], 0, 0)

    return [pl.BlockSpec((1, 1, ATTN_HALF, PAGE_SIZE), functools.partial(page_map, i)) for i in range(g)]


def _fox_decode(page_table, li, q, k_new, v_new, d_new, d_past, cache_k, cache_v):
    db, t, _ = q.shape
    n_pages = page_table.shape[1]
    g = PAGES_PER_STEP
    sample = lambda n, j, pt: (n, 0, 0)
    new_rows = pl.BlockSpec((1, ATTN_HALF, PAGE_SIZE), sample)
    in_specs = [pl.BlockSpec((1, t, ATTN_HALF), sample), new_rows, new_rows,
                pl.BlockSpec((1, FOX_HEADS, PAGE_SIZE), sample),
                pl.BlockSpec((1, g, FOX_HEADS, PAGE_SIZE), lambda n, j, pt: (n, j, 0, 0))]
    in_specs += _page_specs(li, g) * 2
    grid_spec = pltpu.PrefetchScalarGridSpec(
        num_scalar_prefetch=1, grid=(db, n_pages // g), in_specs=in_specs,
        out_specs=pl.BlockSpec((1, t, ATTN_HALF), sample),
        scratch_shapes=[pltpu.VMEM((ROWS, ATTN_HALF), BF16), pltpu.VMEM((ROWS, 1), F32),
                        pltpu.VMEM((1, ROWS, 1), F32), pltpu.VMEM((1, ROWS, 1), F32),
                        pltpu.VMEM((ROWS, ATTN_HALF), F32)])
    return pl.pallas_call(
        functools.partial(_fox_decode_kernel, g=g),
        grid_spec=grid_spec,
        out_shape=jax.ShapeDtypeStruct((db, t, ATTN_HALF), F32),
        compiler_params=_cparams(2),
        name="fox_decode",
    )(page_table, q, k_new, v_new, d_new, d_past, *([cache_k] * g), *([cache_v] * g))


def _moba_decode_kernel(pt_ref, q_ref, qt_ref, kn_ref, vn_ref, *refs, g, n_blocks):
    k_refs, v_refs = refs[:g], refs[g:2 * g]
    o_ref, qb_ref, qft_ref, mb_ref, lb_ref, gb_ref, ob_ref, m_ref, l_ref = refs[2 * g:]
    j = pl.program_id(1)
    lane = _iota((ROWS, LANES), 1)
    row = _iota((ROWS, LANES), 0)
    tok = row // MOBA_HEADS
    pages_per_block = MOBA_BLOCK // PAGE_SIZE

    @pl.when(j == 0)
    def _():
        qb_ref[...] = (_block_diag_queries(q_ref[0]) * ATTN_SCALE).astype(BF16)
        qt = qt_ref[0]
        r_id = _iota(qt.shape, 1)
        c_id = _iota(qt.shape, 0)
        rep = jnp.zeros(qt.shape, F32)
        for t in range(ROWS // MOBA_HEADS):
            rep = jnp.where((r_id // MOBA_HEADS) == t, qt[:, t:t + 1], rep)
        qft_ref[...] = jnp.where(((c_id // HEAD_DIM) == (r_id % MOBA_HEADS)) & (r_id < ROWS), rep, 0.0)
        mb_ref[...] = jnp.zeros_like(mb_ref)
        lb_ref[...] = jnp.zeros_like(lb_ref)
        gb_ref[...] = jnp.zeros_like(gb_ref)

    blocks_per_step = g // pages_per_block
    scores = [_dot(qb_ref[...], _page_bf16(kp)) for kp in k_refs]
    mb, lb, gb = mb_ref[...], lb_ref[...], gb_ref[...]
    for bl in range(blocks_per_step):
        blk = j * blocks_per_step + bl
        pages = range(bl * pages_per_block, (bl + 1) * pages_per_block)
        kmean = jnp.sum(sum(k_refs[i][0, 0] for i in pages), axis=1, keepdims=True) * (1.0 / MOBA_BLOCK)
        gate_row = jnp.sum(qft_ref[...] * kmean, axis=0, keepdims=True)
        gate = jnp.sum(jnp.where(row == lane, gate_row, 0.0), axis=1, keepdims=True)
        m_b = functools.reduce(jnp.maximum, [jnp.max(scores[i], axis=1, keepdims=True) for i in pages])
        ps = [jnp.exp(scores[i] - m_b) for i in pages]
        l_b = sum(jnp.sum(p, axis=1, keepdims=True) for p in ps)
        ob_ref[blk] = sum(_dot_nt(p.astype(BF16), _page_bf16(v_refs[i])) for p, i in zip(ps, pages))
        here = lane == blk
        mb = jnp.where(here, m_b, mb)
        lb = jnp.where(here, l_b, lb)
        gb = jnp.where(here, gate, gb)
    mb_ref[...], lb_ref[...], gb_ref[...] = mb, lb, gb

    @pl.when(j == pl.num_programs(1) - 1)
    def _():
        m_ref[...] = jnp.full(m_ref.shape, NEG_INF, F32)
        l_ref[...] = jnp.zeros_like(l_ref)
        s_own = _dot(qb_ref[...], kn_ref[0].astype(BF16))
        p_own, _ = _softmax_step(s_own, lane <= tok, 0, m_ref, l_ref, 1)
        o_own = _dot_nt(p_own.astype(BF16), vn_ref[0].astype(BF16))
        m_own, l_own = m_ref[0], l_ref[0]
        valid = lane < n_blocks
        gate = jnp.where(valid, gb_ref[...], NEG_INF)
        rank = jnp.zeros((ROWS, LANES), jnp.int32)
        for b in range(n_blocks):
            gj = gate[:, b:b + 1]
            rank = rank + ((gj > gate) | ((gj == gate) & (lane > b))).astype(jnp.int32)
        sel = (rank < MOBA_TOPK) & valid
        m_all = jnp.maximum(jnp.max(jnp.where(sel, mb_ref[...], NEG_INF), axis=1, keepdims=True), m_own)
        wgt = jnp.where(sel, jnp.exp(jnp.where(sel, mb_ref[...] - m_all, 0.0)), 0.0)
        w_own = jnp.exp(m_own - m_all)
        den = jnp.sum(wgt * lb_ref[...], axis=1, keepdims=True) + w_own * l_own
        num = w_own * o_own
        for b in range(n_blocks):
            num = num + wgt[:, b:b + 1] * ob_ref[b]
        o_ref[0] = _heads_to_tokens(num / den)


def _moba_decode(page_table, li, q, q_t, k_new, v_new, cache_k, cache_v):
    db, t, _ = q.shape
    n_pages = page_table.shape[1]
    g = PAGES_PER_STEP
    n_blocks = n_pages * PAGE_SIZE // MOBA_BLOCK
    sample = lambda n, j, pt: (n, 0, 0)
    new_rows = pl.BlockSpec((1, ATTN_HALF, PAGE_SIZE), sample)
    tokens = pl.BlockSpec((1, t, ATTN_HALF), sample)
    grid_spec = pltpu.PrefetchScalarGridSpec(
        num_scalar_prefetch=1, grid=(db, n_pages // g),
        in_specs=[tokens, new_rows, new_rows, new_rows] + _page_specs(li, g) * 2, out_specs=tokens,
        scratch_shapes=[pltpu.VMEM((ROWS, ATTN_HALF), BF16), pltpu.VMEM((ATTN_HALF, LANES), F32),
                        pltpu.VMEM((ROWS, LANES), F32), pltpu.VMEM((ROWS, LANES), F32),
                        pltpu.VMEM((ROWS, LANES), F32), pltpu.VMEM((n_blocks, ROWS, ATTN_HALF), F32),
                        pltpu.VMEM((1, ROWS, 1), F32), pltpu.VMEM((1, ROWS, 1), F32)])
    return pl.pallas_call(
        functools.partial(_moba_decode_kernel, g=g, n_blocks=n_blocks),
        grid_spec=grid_spec,
        out_shape=jax.ShapeDtypeStruct((db, t, ATTN_HALF), F32),
        compiler_params=_cparams(2),
        name="moba_decode",
    )(page_table, q, q_t, k_new, v_new, *([cache_k] * g), *([cache_v] * g))


def _pool_kernel(u_ref, ctx_ref, w_ref, sc_ref, o_ref, ext_ref, *, tm, pos0):
    i = pl.program_id(1)

    @pl.when(i == 0)
    def _():
        ext_ref[0:POOL_HALO, :] = ctx_ref[0]

    @pl.when(i > 0)
    def _():
        ext_ref[0:POOL_HALO, :] = ext_ref[tm:tm + POOL_HALO, :]

    ext_ref[POOL_HALO:, :] = u_ref[0]
    pos = pos0 + i * tm + _iota((tm, 1), 0)
    for gi, win in enumerate(POOL_WINDOWS):
        cols = slice(gi * POOL_GROUP_WIDTH, (gi + 1) * POOL_GROUP_WIDTH)
        x = ext_ref[:, cols]
        tot = x
        shift = 1
        while shift < win:
            tot = tot + pltpu.roll(tot, shift, 0)
            shift *= 2
        cnt = jnp.minimum(win, pos + 1).astype(F32)
        d = tot[POOL_HALO:] / cnt - x[POOL_HALO:]
        y = _dot(d.astype(BF16), w_ref[gi])
        o_ref[0, :, cols] = y * sc_ref[:, cols]


def _pool_mix(u, ctx, w_pool, pool_scale, pos0):
    n, l, _ = u.shape
    tm = min(512, l)
    return pl.pallas_call(
        functools.partial(_pool_kernel, tm=tm, pos0=pos0),
        grid=(n, l // tm),
        in_specs=[pl.BlockSpec((1, tm, POOL_WIDTH), lambda b, i: (b, i, 0)),
                  pl.BlockSpec((1, POOL_HALO, POOL_WIDTH), lambda b, i: (b, 0, 0)),
                  pl.BlockSpec((len(POOL_WINDOWS), POOL_GROUP_WIDTH, POOL_GROUP_WIDTH), lambda b, i: (0, 0, 0)),
                  pl.BlockSpec((1, POOL_WIDTH), lambda b, i: (0, 0))],
        out_specs=pl.BlockSpec((1, tm, POOL_WIDTH), lambda b, i: (b, i, 0)),
        out_shape=jax.ShapeDtypeStruct((n, l, POOL_WIDTH), F32),
        scratch_shapes=[pltpu.VMEM((POOL_HALO + tm, POOL_WIDTH), F32)],
        compiler_params=_cparams(2),
        name="pool_mix",
    )(u, ctx, w_pool, pool_scale)


def _gdn_prep_kernel(x_ref, ctx_ref, ab_ref, cw_ref, alog_ref, dtb_ref, q_ref, k_ref, v_ref, gb_ref, ext_ref, *, tm):
    i = pl.program_id(1)

    @pl.when(i == 0)
    def _():
        ext_ref[0:CONV_HALO, :] = ctx_ref[0]

    @pl.when(i > 0)
    def _():
        ext_ref[0:CONV_HALO, :] = ext_ref[tm:tm + CONV_HALO, :]

    ext_ref[CONV_HALO:, :] = x_ref[0]
    for part, out_ref in enumerate((q_ref, k_ref, v_ref)):
        for h in range(GDN_HEADS):
            c0 = part * GDN_WIDTH + h * GDN_HEAD_DIM
            x = ext_ref[:, c0:c0 + GDN_HEAD_DIM]
            y = x[CONV_HALO:] * cw_ref[CONV_WIDTH - 1:CONV_WIDTH, c0:c0 + GDN_HEAD_DIM]
            for tap in range(CONV_WIDTH - 1):
                shifted = pltpu.roll(x, CONV_CTX - tap, 0)[CONV_HALO:]
                y = y + shifted * cw_ref[tap:tap + 1, c0:c0 + GDN_HEAD_DIM]
            y = y * _sigmoid(y)
            if part < 2:
                y = y * lax.rsqrt(jnp.sum(y * y, axis=-1, keepdims=True) + NORM_EPS)
            if part == 0:
                y = y * GDN_SCALE
            out_ref[0, :, h * GDN_HEAD_DIM:(h + 1) * GDN_HEAD_DIM] = y
    ab = ab_ref[0]
    g = -jnp.exp(alog_ref[...]) * _softplus(ab + dtb_ref[...])
    beta = _sigmoid(ab)
    gb_ref[0] = jnp.where(_iota(ab.shape, 1) < GDN_HEADS, g, beta)


def _gdn_prep(qkv, ctx, ab, conv_w, a_log_row, dt_bias_row):
    n, l, width = qkv.shape
    tm = min(256, l)
    outs = [jax.ShapeDtypeStruct((n, l, GDN_WIDTH), F32)] * 3 + [jax.ShapeDtypeStruct((n, l, LANES), F32)]
    row = lambda b, i: (b, i, 0)
    fixed = lambda b, i: (0, 0)
    return pl.pallas_call(
        functools.partial(_gdn_prep_kernel, tm=tm),
        grid=(n, l // tm),
        in_specs=[pl.BlockSpec((1, tm, width), row),
                  pl.BlockSpec((1, CONV_HALO, width), lambda b, i: (b, 0, 0)),
                  pl.BlockSpec((1, tm, LANES), row),
                  pl.BlockSpec((CONV_WIDTH, width), fixed),
                  pl.BlockSpec((1, LANES), fixed),
                  pl.BlockSpec((1, LANES), fixed)],
        out_specs=[pl.BlockSpec((1, tm, GDN_WIDTH), row)] * 3 + [pl.BlockSpec((1, tm, LANES), row)],
        out_shape=outs,
        scratch_shapes=[pltpu.VMEM((CONV_HALO + tm, width), F32)],
        compiler_params=_cparams(2),
        name="gdn_prep",
    )(qkv, ctx, ab, conv_w, a_log_row, dt_bias_row)


def _gdn_chunk_kernel(q_ref, k_ref, v_ref, gb_ref, u_ref, w_ref, qg_ref, kd_ref, at_ref, eg_ref, *, c, chunks):
    ri = _iota((c, c), 0)
    ci = _iota((c, c), 1)
    eye = ri == ci
    incl = ri >= ci
    strict = ri > ci
    eye_f = eye.astype(F32)
    levels = max(1, (c - 1).bit_length())
    pairs = [(ch, h) for ch in range(chunks) for h in range(GDN_HEADS)]
    rows = lambda ch: slice(ch * c, (ch + 1) * c)
    cols = lambda h: slice(h * GDN_HEAD_DIM, (h + 1) * GDN_HEAD_DIM)
    ks = [k_ref[0, rows(ch), cols(h)] for ch, h in pairs]
    g_cols = [gb_ref[0, rows(ch), h:h + 1] for ch, h in pairs]
    betas = [gb_ref[0, rows(ch), GDN_HEADS + h:GDN_HEADS + h + 1] for ch, h in pairs]
    gc_cols, decays = [], []
    for g_col in g_cols:
        g_row = jnp.sum(jnp.where(eye, g_col, 0.0), axis=0, keepdims=True)
        gc_col = jnp.sum(jnp.where(incl, g_row, 0.0), axis=1, keepdims=True)
        gc_row = jnp.sum(jnp.where(ri <= ci, g_col, 0.0), axis=0, keepdims=True)
        gc_cols.append(gc_col)
        decays.append(jnp.where(incl, jnp.exp(jnp.where(incl, gc_col - gc_row, 0.0)), 0.0))
    kbetas = [k * beta for k, beta in zip(ks, betas)]
    a_mats = [jnp.where(strict, _mm_nt(kb, k) * d, 0.0) for kb, k, d in zip(kbetas, ks, decays)]
    ts = [eye_f - a for a in a_mats]
    npows = [_mm(a, a) for a in a_mats]
    for lvl in range(1, levels):
        ts = [t + _mm(t, npow) for t, npow in zip(ts, npows)]
        if lvl < levels - 1:
            npows = [_mm(npow, npow) for npow in npows]
    egs = [jnp.exp(gc) for gc in gc_cols]
    for idx, (ch, h) in enumerate(pairs):
        r, cl = rows(ch), cols(h)
        q, k, gc = q_ref[0, r, cl], ks[idx], gc_cols[idx]
        u_ref[0, r, cl] = _mm(ts[idx], v_ref[0, r, cl] * betas[idx])
        w_ref[0, r, cl] = _mm(ts[idx], kbetas[idx] * egs[idx])
        qg_ref[0, r, cl] = q * egs[idx]
        kd_ref[0, r, cl] = k * jnp.exp(gc[c - 1:c, :] - gc)
        at_ref[0, r, h * c:(h + 1) * c] = jnp.where(incl, _mm_nt(q, k) * decays[idx], 0.0)
    lane = _iota((c, LANES), 1)
    for ch in range(chunks):
        eg_full = jnp.zeros((c, LANES), F32)
        for h in range(GDN_HEADS):
            eg_full = jnp.where(lane == h, egs[ch * GDN_HEADS + h], eg_full)
        eg_ref[0, rows(ch), :] = eg_full


def _gdn_chunks(q, k, v, gb, c):
    n, l, _ = q.shape
    chunks = max(1, min(4, l // c))
    tm = chunks * c
    row = lambda b, i: (b, i, 0)
    big = jax.ShapeDtypeStruct((n, l, GDN_WIDTH), F32)
    outs = [big] * 4 + [jax.ShapeDtypeStruct((n, l, GDN_HEADS * c), F32), jax.ShapeDtypeStruct((n, l, LANES), F32)]
    return pl.pallas_call(
        functools.partial(_gdn_chunk_kernel, c=c, chunks=chunks),
        grid=(n, l // tm),
        in_specs=[pl.BlockSpec((1, tm, GDN_WIDTH), row)] * 3 + [pl.BlockSpec((1, tm, LANES), row)],
        out_specs=[pl.BlockSpec((1, tm, GDN_WIDTH), row)] * 4
        + [pl.BlockSpec((1, tm, GDN_HEADS * c), row), pl.BlockSpec((1, tm, LANES), row)],
        out_shape=outs,
        compiler_params=_cparams(2),
        name="gdn_chunks",
    )(q, k, v, gb)


def _gdn_scan_kernel(u_ref, w_ref, qg_ref, kd_ref, at_ref, eg_ref, z_ref, s0_ref, nw_ref, o_ref, sout_ref, s_ref, *, c):
    i = pl.program_id(1)

    @pl.when(i == 0)
    def _():
        s_ref[...] = s0_ref[0]

    heads = range(GDN_HEADS)
    cols = lambda h: slice(h * GDN_HEAD_DIM, (h + 1) * GDN_HEAD_DIM)
    states = [s_ref[h] for h in heads]
    v_news = [u_ref[0, :, cols(h)] - _mm(w_ref[0, :, cols(h)], states[h]) for h in heads]
    outs = [_mm(qg_ref[0, :, cols(h)], states[h]) + _mm(at_ref[0, :, h * c:(h + 1) * c], v_news[h]) for h in heads]
    for h in heads:
        s_ref[h] = states[h] * eg_ref[0, c - 1:c, h:h + 1] + _mm_tn(kd_ref[0, :, cols(h)], v_news[h])
        o = outs[h]
        o = o * lax.rsqrt(jnp.mean(o * o, axis=-1, keepdims=True) + NORM_EPS) * nw_ref[...]
        z = z_ref[0, :, cols(h)]
        o_ref[0, :, cols(h)] = o * (z * _sigmoid(z))

    @pl.when(i == pl.num_programs(1) - 1)
    def _():
        sout_ref[0] = s_ref[...]


def _gdn_scan(u, w, qg, kd, at, eg, z, s0, norm_w, c):
    n, l, _ = u.shape
    row = lambda b, i: (b, i, 0)
    state = lambda b, i: (b, 0, 0, 0)
    st_shape = (1, GDN_HEADS, GDN_HEAD_DIM, GDN_HEAD_DIM)
    return pl.pallas_call(
        functools.partial(_gdn_scan_kernel, c=c),
        grid=(n, l // c),
        in_specs=[pl.BlockSpec((1, c, GDN_WIDTH), row)] * 4
        + [pl.BlockSpec((1, c, GDN_HEADS * c), row), pl.BlockSpec((1, c, LANES), row),
           pl.BlockSpec((1, c, GDN_WIDTH), row), pl.BlockSpec(st_shape, state),
           pl.BlockSpec((1, GDN_HEAD_DIM), lambda b, i: (0, 0))],
        out_specs=[pl.BlockSpec((1, c, GDN_WIDTH), row), pl.BlockSpec(st_shape, state)],
        out_shape=[jax.ShapeDtypeStruct((n, l, GDN_WIDTH), F32),
                   jax.ShapeDtypeStruct((n,) + st_shape[1:], F32)],
        scratch_shapes=[pltpu.VMEM(st_shape[1:], F32)],
        compiler_params=_cparams(2),
        name="gdn_scan",
    )(u, w, qg, kd, at, eg, z, s0, norm_w)


def _pad_lanes(x, width=LANES):
    return jnp.pad(x, [(0, 0)] * (x.ndim - 1) + [(0, width - x.shape[-1])])


def _even_weights(w_in, b_f):
    fq, fk, fv, fl, mq, mk, mv = jnp.split(
        w_in, [ATTN_HALF, 2 * ATTN_HALF, 3 * ATTN_HALF, 3 * ATTN_HALF + FOX_HEADS,
               4 * ATTN_HALF + FOX_HEADS, 5 * ATTN_HALF + FOX_HEADS], axis=1)
    w_main = jnp.concatenate([fq, fk, fv, mq, mk, mv], axis=1).astype(BF16)
    return w_main, _pad_lanes(fl).astype(BF16), _pad_lanes(b_f[None, :])


def _odd_weights(w_in):
    u, qkv, a, b, z = jnp.split(
        w_in, [POOL_WIDTH, POOL_WIDTH + 3 * GDN_WIDTH, POOL_WIDTH + 3 * GDN_WIDTH + GDN_HEADS,
               POOL_WIDTH + 3 * GDN_WIDTH + 2 * GDN_HEADS], axis=1)
    w_main = jnp.concatenate([u, qkv, z], axis=1).astype(BF16)
    return w_main, _pad_lanes(jnp.concatenate([a, b], axis=1)).astype(BF16)


def _odd_mixer(h, n, l, pool_ctx, conv_ctx, s0, pos0, chunk, w_main, w_small, conv_w, a_log_row, dt_bias_row,
               norm_w, w_pool, pool_scale):
    u, qkv, z, ab = _proj(h, w_main, w_small, jnp.zeros((1, LANES), F32),
                          (POOL_WIDTH, 3 * GDN_WIDTH, GDN_WIDTH), False)
    u3, qkv3, z3, ab3 = (t.reshape(n, l, -1) for t in (u, qkv, z, ab))
    lp = -(-l // SUBLANES) * SUBLANES
    padr = lambda t: jnp.pad(t, ((0, 0), (0, lp - l), (0, 0)))
    ctx_p = jnp.pad(pool_ctx, ((0, 0), (POOL_HALO - POOL_CTX, 0), (0, 0)))
    ctx_c = jnp.pad(conv_ctx, ((0, 0), (CONV_HALO - CONV_CTX, 0), (0, 0)))
    y_pool = _pool_mix(padr(u3), ctx_p, w_pool, pool_scale, pos0)[:, :l]
    q, k, v, gb = _gdn_prep(padr(qkv3), ctx_c, padr(ab3), conv_w, a_log_row, dt_bias_row)
    if lp != l:
        gb = gb * (jnp.arange(lp) < l).astype(F32)[None, :, None]
    if chunk % SUBLANES:
        assert chunk == l
        chunk = lp
    c = chunk
    uu, ww, qg, kd, at, eg = _gdn_chunks(q, k, v, gb, c)
    o, s_new = _gdn_scan(uu, ww, qg, kd, at, eg, padr(z3), s0, norm_w, c)
    new_pool = jnp.concatenate([pool_ctx, u3], axis=1)[:, -POOL_CTX:]
    new_conv = jnp.concatenate([conv_ctx, qkv3], axis=1)[:, -CONV_CTX:]
    return (y_pool.reshape(n * l, POOL_WIDTH), o[:, :l].reshape(n * l, GDN_WIDTH), new_pool, new_conv, s_new)


def kernel(x_prompt, x_sample, cache_fox_k, cache_fox_v, cache_fox_logf, cache_moba_k, cache_moba_v, state_pool, state_conv, state_gdn, page_table, w_in_even, b_forget, w_out_even, w_in_odd, conv_w, a_log, dt_bias, gdn_norm_w, w_pool, pool_scale, w_out_odd, ln1_g, ln1_b, ln2_g, ln2_b, w_gate_up, w_down):
    B, S, _ = x_prompt.shape
    DB, T, _ = x_sample.shape
    P = page_table.shape[1] * PAGE_SIZE
    hp = x_prompt.reshape(B * S, D_MODEL)
    hs = x_sample.reshape(DB * T, D_MODEL)

    assert T * FOX_HEADS == ROWS and P % MOBA_BLOCK == 0
    cfl_t = jnp.swapaxes(cache_fox_logf, 2, 3)
    paged_t = lambda c: jnp.transpose(c, (0, 1, 3, 4, 2)).reshape(c.shape[0], c.shape[1], ATTN_HALF, PAGE_SIZE)
    cfk, cfv, cmk, cmv = paged_t(cache_fox_k), paged_t(cache_fox_v), paged_t(cache_moba_k), paged_t(cache_moba_v)
    new_t = lambda t: jnp.pad(jnp.swapaxes(t.reshape(DB, T, ATTN_HALF), 1, 2), ((0, 0), (0, 0), (0, PAGE_SIZE - T)))

    outs_p = {k: [] for k in ("fk", "fv", "fl", "mk", "mv", "pool", "conv", "gdn")}
    outs_s = {k: [] for k in ("fk", "fv", "fl", "mk", "mv", "pool", "conv", "gdn")}
    widths_even = (ATTN_HALF,) * 6
    for layer in range(DEPTH):
        li = layer // 2
        row = lambda a: a[layer][None, :]
        if layer % 2 == 0:
            w_main, w_fl, b_fl = _even_weights(w_in_even[li], b_forget[li])
            w_out = w_out_even[li].astype(BF16)
            fq, fk, fv, mq, mk, mv, fl = _proj(hp, w_main, w_fl, b_fl, widths_even, True)
            b3 = lambda t: t.reshape(B, S, ATTN_HALF)
            logf = fl[:, :FOX_HEADS].reshape(B, S, FOX_HEADS)
            c_rows = _cumsum_rows(jnp.swapaxes(logf, 1, 2)).reshape(B, FOX_HEADS // 2, 2, S)
            fox = _prompt_attn(b3(fq), b3(fk), b3(fv), c_rows, False)
            moba = _prompt_attn(b3(mq), b3(mk), b3(mv), c_rows, True)
            a1_p, a2_p = fox.reshape(B * S, ATTN_HALF), moba.reshape(B * S, ATTN_HALF)
            hd = lambda t, n, l: t.reshape(n, l, FOX_HEADS, HEAD_DIM)
            for key, val in (("fk", hd(fk, B, S)), ("fv", hd(fv, B, S)), ("fl", logf),
                             ("mk", hd(mk, B, S)), ("mv", hd(mv, B, S))):
                outs_p[key].append(val)
            fq, fk, fv, mq, mk, mv, fl = _proj(hs, w_main, w_fl, b_fl, widths_even, True)
            logf = fl[:, :FOX_HEADS].reshape(DB, T, FOX_HEADS)
            l_new = _pad_lanes(jnp.swapaxes(logf, 1, 2), PAGE_SIZE)
            d_new, d_past = _decay(page_table, li, l_new, cfl_t)
            tok = lambda t: t.reshape(DB, T, ATTN_HALF)
            fox = _fox_decode(page_table, li, tok(fq), new_t(fk), new_t(fv), d_new, d_past, cfk, cfv)
            moba = _moba_decode(page_table, li, tok(mq), new_t(mq), new_t(mk), new_t(mv), cmk, cmv)
            a1_s, a2_s = fox.reshape(DB * T, ATTN_HALF), moba.reshape(DB * T, ATTN_HALF)
            for key, val in (("fk", hd(fk, DB, T)), ("fv", hd(fv, DB, T)), ("fl", logf),
                             ("mk", hd(mk, DB, T)), ("mv", hd(mv, DB, T))):
                outs_s[key].append(val)
            w1, w2 = w_out[:ATTN_HALF], w_out[ATTN_HALF:]
        else:
            w_main, w_small = _odd_weights(w_in_odd[li])
            w_out = w_out_odd[li].astype(BF16)
            shared = (w_main, w_small, conv_w[li], _pad_lanes(a_log[li][None, :]), _pad_lanes(dt_bias[li][None, :]),
                      gdn_norm_w[li][None, :], w_pool[li].astype(BF16), pool_scale[li][None, :])
            a1_p, a2_p, sp, sc, sg = _odd_mixer(
                hp, B, S, jnp.zeros((B, POOL_CTX, POOL_WIDTH), F32), jnp.zeros((B, CONV_CTX, 3 * GDN_WIDTH), F32),
                jnp.zeros((B, GDN_HEADS, GDN_HEAD_DIM, GDN_HEAD_DIM), F32), 0, GDN_CHUNK, *shared)
            outs_p["pool"].append(sp); outs_p["conv"].append(sc); outs_p["gdn"].append(sg)
            a1_s, a2_s, sp, sc, sg = _odd_mixer(
                hs, DB, T, state_pool[li], state_conv[li], state_gdn[li], P, T, *shared)
            outs_s["pool"].append(sp); outs_s["conv"].append(sc); outs_s["gdn"].append(sg)
            w1, w2 = w_out[:POOL_WIDTH], w_out[POOL_WIDTH:]
        wgu, wdn = w_gate_up[layer].astype(BF16), w_down[layer].astype(BF16)
        hp = _outproj_ln(a1_p, a2_p, hp, w1, w2, row(ln1_g), row(ln1_b))
        hp = _ffn_ln(hp, wgu, wdn, row(ln2_g), row(ln2_b))
        hs = _outproj_ln(a1_s, a2_s, hs, w1, w2, row(ln1_g), row(ln1_b))
        hs = _ffn_ln(hs, wgu, wdn, row(ln2_g), row(ln2_b))

    st = jnp.stack
    keys = ("fk", "fv", "fl", "mk", "mv", "pool", "conv", "gdn")
    return ((hp.reshape(B, S, D_MODEL), hs.reshape(DB, T, D_MODEL))
            + tuple(st(outs_p[k]) for k in keys) + tuple(st(outs_s[k]) for k in keys))
```

```python
import functools

import jax
import jax.numpy as jnp
from jax import lax
from jax.experimental import pallas as pl
from jax.experimental.pallas import tpu as pltpu

F32 = jnp.float32
BF16 = jnp.bfloat16
HI = lax.Precision.HIGHEST

D_MODEL = 1024
HEAD_DIM = 64
FOX_HEADS = 8
MOBA_HEADS = 8
ATTN_HALF = FOX_HEADS * HEAD_DIM
ATTN_SCALE = HEAD_DIM ** -0.5
MOBA_BLOCK = 256
MOBA_TOPK = 3
PAGE_SIZE = 128
POOL_WINDOWS = (2, 4, 8, 16)
POOL_WIDTH = 512
POOL_GROUP_WIDTH = 128
POOL_CTX = 15
GDN_HEADS = 4
GDN_HEAD_DIM = 128
GDN_WIDTH = GDN_HEADS * GDN_HEAD_DIM
GDN_SCALE = GDN_HEAD_DIM ** -0.5
GDN_CHUNK = 64
CONV_WIDTH = 4
CONV_CTX = CONV_WIDTH - 1
DEPTH = 4
DEEPNORM_ALPHA = (2 * DEPTH) ** 0.25
LN_EPS = 1e-5
NORM_EPS = 1e-6
NEG_INF = -1e30

LANES = 128
SUBLANES = 8
POOL_HALO = 16
CONV_HALO = 8
VMEM_LIMIT = 56 * 1024 * 1024


def _cparams(n_axes):
    return pltpu.CompilerParams(dimension_semantics=("arbitrary",) * n_axes,
                                vmem_limit_bytes=VMEM_LIMIT)


def _dot(a, b, precision=lax.Precision.DEFAULT):
    return jnp.dot(a, b, preferred_element_type=F32, precision=precision)


def _dot_split3(a, b):
    hi = a.astype(BF16)
    mid = (a - hi.astype(F32)).astype(BF16)
    lo = (a - hi.astype(F32) - mid.astype(F32)).astype(BF16)
    return _dot(hi, b) + _dot(mid, b) + _dot(lo, b)


def _dot_nt(a, b, precision=lax.Precision.DEFAULT):
    return lax.dot_general(a, b, (((1,), (1,)), ((), ())), preferred_element_type=F32, precision=precision)


def _split2(a):
    hi = a.astype(BF16)
    return hi, (a - hi.astype(F32)).astype(BF16)


def _mm3(a, b, dims):
    ah, al = _split2(a)
    bh, bl = _split2(b)
    dg = lambda x, y: lax.dot_general(x, y, dims, preferred_element_type=F32)
    return dg(ah, bh) + dg(ah, bl) + dg(al, bh)


def _mm(a, b):
    return _mm3(a, b, (((1,), (0,)), ((), ())))


def _mm_nt(a, b):
    return _mm3(a, b, (((1,), (1,)), ((), ())))


def _mm_tn(a, b):
    return _mm3(a, b, (((0,), (0,)), ((), ())))


def _sigmoid(x):
    return 1.0 / (1.0 + jnp.exp(-x))


def _softplus(x):
    return jnp.maximum(x, 0.0) + jnp.log1p(jnp.exp(-jnp.abs(x)))


def _layer_norm(y, g, b):
    mu = jnp.mean(y, axis=-1, keepdims=True)
    yc = y - mu
    var = jnp.mean(yc * yc, axis=-1, keepdims=True)
    return yc * lax.rsqrt(var + LN_EPS) * g + b


def _iota(shape, axis):
    return lax.broadcasted_iota(jnp.int32, shape, axis)


def _proj_kernel(x_ref, w_ref, wt_ref, ws_ref, bs_ref, *refs, widths, transposed, log_sigmoid, n_carried):
    out_refs = refs[n_carried:]
    xb = x_ref[...].astype(BF16)
    off = 0
    for o_ref, wd, tr in zip(out_refs[:-1], widths, transposed):
        if tr:
            o_ref[0, 0] = _dot_nt(wt_ref[off:off + wd, :], xb)
        else:
            o_ref[...] = _dot(xb, w_ref[:, off:off + wd])
        off += wd
    small = _dot(xb, ws_ref[...]) + bs_ref[...]
    if log_sigmoid:
        small = -_softplus(-small)
    out_refs[-1][...] = small


def _proj(x, w_main, w_small, b_small, widths, log_sigmoid, transposed=None, batch=1, layer=(0, 1), carried=()):
    m = x.shape[0]
    tm = min(512, m)
    n_main = w_main.shape[1]
    transposed = transposed or (False,) * len(widths)
    li, n_layers = layer
    seq = m // batch
    per_batch = seq // tm
    w_t = w_main.T if any(transposed) else jnp.zeros((SUBLANES, D_MODEL), BF16)
    outs, out_specs, t_slots = [], [], []
    for wd, tr in zip(widths, transposed):
        if tr:
            t_slots.append(len(outs))
            outs.append(jax.ShapeDtypeStruct((n_layers, batch, wd, seq), F32))
            out_specs.append(pl.BlockSpec((1, 1, wd, tm), lambda i: (li, i // per_batch, 0, i % per_batch)))
        else:
            outs.append(jax.ShapeDtypeStruct((m, wd), F32))
            out_specs.append(pl.BlockSpec((tm, wd), lambda i: (i, 0)))
    outs.append(jax.ShapeDtypeStruct((m, LANES), F32))
    out_specs.append(pl.BlockSpec((tm, LANES), lambda i: (i, 0)))
    n_fixed = 5
    return pl.pallas_call(
        functools.partial(_proj_kernel, widths=widths, transposed=transposed, log_sigmoid=log_sigmoid,
                          n_carried=len(carried)),
        grid=(m // tm,),
        in_specs=[pl.BlockSpec((tm, D_MODEL), lambda i: (i, 0)),
                  pl.BlockSpec((D_MODEL, n_main), lambda i: (0, 0)),
                  pl.BlockSpec(w_t.shape, lambda i: (0, 0)),
                  pl.BlockSpec((D_MODEL, LANES), lambda i: (0, 0)),
                  pl.BlockSpec((1, LANES), lambda i: (0, 0))]
        + [pl.BlockSpec(memory_space=pl.ANY)] * len(carried),
        out_specs=out_specs,
        out_shape=outs,
        input_output_aliases={n_fixed + k: t_slots[k] for k in range(len(carried))},
        compiler_params=_cparams(1),
        name="in_proj",
    )(x, w_main, w_t, w_small, b_small, *carried)


def _outproj_ln_kernel(a1_ref, a2_ref, x_ref, w1_ref, w2_ref, g_ref, b_ref, o_ref):
    mix = _dot(a1_ref[...].astype(BF16), w1_ref[...]) + _dot(a2_ref[...].astype(BF16), w2_ref[...])
    o_ref[...] = _layer_norm(DEEPNORM_ALPHA * x_ref[...] + mix, g_ref[...], b_ref[...])


def _outproj_ln(a1, a2, x, w1, w2, g, b):
    m = x.shape[0]
    tm = min(512, m)
    k1, k2 = a1.shape[1], a2.shape[1]
    return pl.pallas_call(
        _outproj_ln_kernel,
        grid=(m // tm,),
        in_specs=[pl.BlockSpec((tm, k1), lambda i: (i, 0)),
                  pl.BlockSpec((tm, k2), lambda i: (i, 0)),
                  pl.BlockSpec((tm, D_MODEL), lambda i: (i, 0)),
                  pl.BlockSpec((k1, D_MODEL), lambda i: (0, 0)),
                  pl.BlockSpec((k2, D_MODEL), lambda i: (0, 0)),
                  pl.BlockSpec((1, D_MODEL), lambda i: (0, 0)),
                  pl.BlockSpec((1, D_MODEL), lambda i: (0, 0))],
        out_specs=pl.BlockSpec((tm, D_MODEL), lambda i: (i, 0)),
        out_shape=jax.ShapeDtypeStruct((m, D_MODEL), F32),
        compiler_params=_cparams(1),
        name="outproj_ln",
    )(a1, a2, x, w1, w2, g, b)


def _ffn_kernel(x_ref, wg_ref, wu_ref, wd_ref, g_ref, b_ref, o_ref, xb_ref, acc_ref):
    j = pl.program_id(1)

    @pl.when(j == 0)
    def _():
        xb_ref[...] = x_ref[...].astype(BF16)
        acc_ref[...] = jnp.zeros_like(acc_ref)

    xb = xb_ref[...]
    gate = _dot(xb, wg_ref[...])
    up = _dot(xb, wu_ref[...])
    act = (gate * _sigmoid(gate) * up).astype(BF16)
    acc_ref[...] += _dot(act, wd_ref[...])

    @pl.when(j == pl.num_programs(1) - 1)
    def _():
        o_ref[...] = _layer_norm(DEEPNORM_ALPHA * x_ref[...] + acc_ref[...], g_ref[...], b_ref[...])


def _ffn_ln(x, w_gate_up, w_down, g, b):
    m = x.shape[0]
    d_ff = w_down.shape[0]
    tm = min(256, m)
    tf = d_ff
    nf = d_ff // tf
    return pl.pallas_call(
        _ffn_kernel,
        grid=(m // tm, nf),
        in_specs=[pl.BlockSpec((tm, D_MODEL), lambda i, j: (i, 0)),
                  pl.BlockSpec((D_MODEL, tf), lambda i, j: (0, j)),
                  pl.BlockSpec((D_MODEL, tf), lambda i, j: (0, j + nf)),
                  pl.BlockSpec((tf, D_MODEL), lambda i, j: (j, 0)),
                  pl.BlockSpec((1, D_MODEL), lambda i, j: (0, 0)),
                  pl.BlockSpec((1, D_MODEL), lambda i, j: (0, 0))],
        out_specs=pl.BlockSpec((tm, D_MODEL), lambda i, j: (i, 0)),
        out_shape=jax.ShapeDtypeStruct((m, D_MODEL), F32),
        scratch_shapes=[pltpu.VMEM((tm, D_MODEL), BF16), pltpu.VMEM((tm, D_MODEL), F32)],
        compiler_params=_cparams(2),
        name="ffn_ln",
    )(x, w_gate_up, w_gate_up, w_down, g, b)


def _cumsum_kernel(x_ref, o_ref, *, blk):
    n = x_ref.shape[-1]
    tri = (_iota((blk, blk), 0) <= _iota((blk, blk), 1)).astype(F32)
    carry = jnp.zeros((x_ref.shape[1], 1), F32)
    for i in range(n // blk):
        y = _dot(x_ref[0, :, i * blk:(i + 1) * blk], tri, HI) + carry
        o_ref[0, :, i * blk:(i + 1) * blk] = y
        carry = y[:, blk - 1:blk]


def _cumsum_rows(x):
    b, h, s = x.shape
    return pl.pallas_call(
        functools.partial(_cumsum_kernel, blk=min(256, s)),
        grid=(b,),
        in_specs=[pl.BlockSpec((1, h, s), lambda i: (i, 0, 0))],
        out_specs=pl.BlockSpec((1, h, s), lambda i: (i, 0, 0)),
        out_shape=jax.ShapeDtypeStruct((b, h, s), F32),
        compiler_params=_cparams(1),
        name="logf_cumsum",
    )(x)


KEY_TILES_PER_UPDATE = 4
LOG2E = 1.4426950408889634


def _softmax_step(s, valid, h, m_ref, l_ref, axis):
    if valid is not None:
        s = jnp.where(valid, s, NEG_INF)
    m_old = m_ref[h]
    m_new = jnp.maximum(m_old, jnp.max(s, axis=axis, keepdims=True))
    alpha = jnp.exp(m_old - m_new)
    p = jnp.exp(s - m_new)
    if valid is not None:
        p = jnp.where(valid, p, 0.0)
    l_ref[h] = alpha * l_ref[h] + jnp.sum(p, axis=axis, keepdims=True)
    m_ref[h] = m_new
    return p, alpha


def _prompt_attn_kernel(q_ref, k_ref, v_ref, c_ref, o_ref, kb_ref, vt_ref, aux_ref, sel_ref,
                        m_ref, l_ref, acc_ref, *, t, moba):
    qi = pl.program_id(2)
    seq = k_ref.shape[3]
    nblk = seq // t
    krow = _iota((t, t), 0)
    qcol = _iota((t, t), 1)

    @pl.when(qi == 0)
    def _():
        vt_ref[...] = v_ref[0, 0].astype(BF16)
        if moba:
            aux_ref[...] = jnp.zeros_like(aux_ref)
        for blk in range(nblk):
            rows = slice(blk * t, (blk + 1) * t)
            k_blk = k_ref[0, 0, :, rows].T
            kb_ref[rows, :] = k_blk.astype(BF16)
            if moba:
                aux_ref[blk:blk + 1, :] = jnp.sum(k_blk, axis=0, keepdims=True) * (1.0 / MOBA_BLOCK)
            else:
                for h in range(2):
                    c_col = jnp.sum(jnp.where(krow == qcol, c_ref[0, 0, h:h + 1, rows], 0.0), axis=1, keepdims=True)
                    aux_ref[h, rows, :] = jnp.broadcast_to(c_col * LOG2E, (t, LANES))

    q = q_ref[0]
    head0 = _iota((t, LANES), 1) < HEAD_DIM
    q_heads = (jnp.where(head0, q, 0.0), jnp.where(head0, 0.0, q))
    qb = tuple((qh * (ATTN_SCALE * LOG2E)).astype(BF16) for qh in q_heads)

    if moba:
        nbp = aux_ref.shape[0]
        blk_id = _iota((nbp, t), 0)
        for h in range(2):
            gate = _dot_nt(aux_ref[...], q_heads[h], HI)
            gate = jnp.where(blk_id < qi, gate, NEG_INF)
            rank = jnp.zeros((nbp, t), jnp.int32)
            for j in range(nblk):
                gj = gate[j:j + 1, :]
                rank = rank + ((gj > gate) | ((gj == gate) & (blk_id > j))).astype(jnp.int32)
            sel = jnp.where((rank < MOBA_TOPK) & (blk_id < qi), 1.0, 0.0)
            for j in range(nblk):
                sel_ref[h, j] = sel[j:j + 1, :]

    m_ref[...] = jnp.full(m_ref.shape, NEG_INF, F32)
    l_ref[...] = jnp.zeros_like(l_ref)
    acc_ref[...] = jnp.zeros_like(acc_ref)

    def tiles(ki0, count, diagonal_last):
        starts = [pl.multiple_of((ki0 + i) * t, t) for i in range(count)]
        for h in range(2):
            scores, valids = [], []
            for i, k0 in enumerate(starts):
                s = _dot_nt(kb_ref[pl.ds(k0, t), :], qb[h])
                diagonal = diagonal_last and i == count - 1
                if moba:
                    valid = (krow <= qcol) if diagonal else (sel_ref[h, ki0 + i] > 0.5)
                else:
                    ck = aux_ref[h, pl.ds(k0, t), :]
                    s = s - jnp.concatenate([ck] * (t // LANES), axis=1)
                    valid = (krow <= qcol) if diagonal else None
                scores.append(s if valid is None else jnp.where(valid, s, NEG_INF))
                valids.append(valid)
            m_old = m_ref[h]
            m_new = functools.reduce(jnp.maximum, [jnp.max(s, axis=0, keepdims=True) for s in scores] + [m_old])
            alpha = jnp.exp2(m_old - m_new)
            rows = slice(h * HEAD_DIM, (h + 1) * HEAD_DIM)
            l_new = alpha * l_ref[h]
            acc = acc_ref[rows, :] * alpha
            for s, valid, k0 in zip(scores, valids, starts):
                p = jnp.exp2(s - m_new)
                if valid is not None:
                    p = jnp.where(valid, p, 0.0)
                l_new = l_new + jnp.sum(p, axis=0, keepdims=True)
                acc = acc + _dot(vt_ref[rows, pl.ds(k0, t)], p.astype(BF16))
            m_ref[h] = m_new
            l_ref[h] = l_new
            acc_ref[rows, :] = acc

    def body(gi, carry):
        tiles(gi * KEY_TILES_PER_UPDATE, KEY_TILES_PER_UPDATE, False)
        return carry

    full = qi // KEY_TILES_PER_UPDATE
    lax.fori_loop(0, full, body, 0)
    for rest in range(KEY_TILES_PER_UPDATE):
        @pl.when(qi - full * KEY_TILES_PER_UPDATE == rest)
        def _():
            tiles(full * KEY_TILES_PER_UPDATE, rest + 1, True)

    out = jnp.concatenate([acc_ref[0:HEAD_DIM, :] / l_ref[0], acc_ref[HEAD_DIM:, :] / l_ref[1]], axis=0)
    o_ref[0] = out.T


def _prompt_attn(q, k, v, c_rows, moba, li):
    b, s, _ = q.shape
    t = MOBA_BLOCK
    nblk = s // t
    n_pairs = ATTN_HALF // LANES
    nbp = -(-nblk // SUBLANES) * SUBLANES
    aux = pltpu.VMEM((nbp, LANES), F32) if moba else pltpu.VMEM((2, s, LANES), F32)
    return pl.pallas_call(
        functools.partial(_prompt_attn_kernel, t=t, moba=moba),
        grid=(b, n_pairs, nblk),
        in_specs=[pl.BlockSpec((1, t, LANES), lambda i, p, j: (i, j, p)),
                  pl.BlockSpec((1, 1, LANES, s), lambda i, p, j: (li, i, p, 0)),
                  pl.BlockSpec((1, 1, LANES, s), lambda i, p, j: (li, i, p, 0)),
                  pl.BlockSpec((1, 1, 2, s), lambda i, p, j: (i, p, 0, 0))],
        out_specs=pl.BlockSpec((1, t, LANES), lambda i, p, j: (i, j, p)),
        out_shape=jax.ShapeDtypeStruct((b, s, ATTN_HALF), F32),
        scratch_shapes=[pltpu.VMEM((s, LANES), BF16), pltpu.VMEM((LANES, s), BF16), aux,
                        pltpu.VMEM((2, nblk, 1, t), F32),
                        pltpu.VMEM((2, 1, t), F32), pltpu.VMEM((2, 1, t), F32),
                        pltpu.VMEM((LANES, t), F32)],
        compiler_params=_cparams(3),
        name="moba_prompt" if moba else "fox_prompt",
    )(q, k, v, c_rows)


PAGES_PER_STEP = 8
DECAY_PAGES_PER_STEP = 32
ROWS = 32


def _decay_kernel(pt_ref, ln_ref, *refs, g):
    l_refs = refs[:g]
    dn_ref, d_ref, carry_ref = refs[g:]
    j = pl.program_id(1)
    later = (_iota((PAGE_SIZE, PAGE_SIZE), 0) > _iota((PAGE_SIZE, PAGE_SIZE), 1)).astype(F32).astype(BF16)

    def page(lt):
        d8 = _dot_split3(lt, later) + carry_ref[...]
        carry_ref[...] = carry_ref[...] + jnp.sum(lt, axis=1, keepdims=True)
        return d8

    @pl.when(j == 0)
    def _():
        carry_ref[...] = jnp.zeros_like(carry_ref)
        dn_ref[0] = page(ln_ref[0])

    for i in range(g):
        d_ref[0, g - 1 - i] = page(l_refs[i][0, 0])


def _decay(page_table, li, l_new, cache_lt):
    db = l_new.shape[0]
    n_pages = page_table.shape[1]
    g = min(DECAY_PAGES_PER_STEP, n_pages)
    steps = n_pages // g

    def page_map(i, n, j, pt):
        return (li, pt[n, n_pages - 1 - (j * g + i)], 0, 0)

    blk = (1, FOX_HEADS, PAGE_SIZE)
    grid_spec = pltpu.PrefetchScalarGridSpec(
        num_scalar_prefetch=1, grid=(db, steps),
        in_specs=[pl.BlockSpec(blk, lambda n, j, pt: (n, 0, 0))]
        + [pl.BlockSpec((1,) + blk, functools.partial(page_map, i)) for i in range(g)],
        out_specs=[pl.BlockSpec(blk, lambda n, j, pt: (n, 0, 0)),
                   pl.BlockSpec((1, g, FOX_HEADS, PAGE_SIZE), lambda n, j, pt: (n, steps - 1 - j, 0, 0))],
        scratch_shapes=[pltpu.VMEM((FOX_HEADS, 1), F32)])
    return pl.pallas_call(
        functools.partial(_decay_kernel, g=g),
        grid_spec=grid_spec,
        out_shape=[jax.ShapeDtypeStruct((db,) + blk[1:], F32),
                   jax.ShapeDtypeStruct((db, n_pages, FOX_HEADS, PAGE_SIZE), F32)],
        compiler_params=_cparams(2),
        name="fox_decay",
    )(page_table, l_new, *([cache_lt] * g))


def _block_diag_queries(q):
    t = q.shape[0]
    rep = jnp.concatenate([jnp.broadcast_to(q[i:i + 1], (FOX_HEADS, ATTN_HALF)) for i in range(t)], axis=0)
    keep = (_iota(rep.shape, 1) // HEAD_DIM) == (_iota(rep.shape, 0) % FOX_HEADS)
    return jnp.where(keep, rep, 0.0)


def _heads_to_tokens(o):
    keep = (_iota(o.shape, 1) // HEAD_DIM) == (_iota(o.shape, 0) % FOX_HEADS)
    o = jnp.where(keep, o, 0.0)
    t = o.shape[0] // FOX_HEADS
    return jnp.sum(o.reshape(t, FOX_HEADS, ATTN_HALF), axis=1)


def _page_bf16(ref):
    return ref[0, 0].astype(BF16)


def _fox_decode_kernel(pt_ref, q_ref, kn_ref, vn_ref, dn_ref, d_ref, *refs, g):
    k_refs, v_refs = refs[:g], refs[g:2 * g]
    o_ref, qb_ref, rc_ref, m_ref, l_ref, acc_ref = refs[2 * g:]
    j = pl.program_id(1)
    lane = _iota((ROWS, PAGE_SIZE), 1)
    tok = _iota((ROWS, PAGE_SIZE), 0) // FOX_HEADS
    per_row = lambda d8: jnp.concatenate([d8] * (ROWS // FOX_HEADS), axis=0)

    @pl.when(j == 0)
    def _():
        qb_ref[...] = (_block_diag_queries(q_ref[0]) * ATTN_SCALE).astype(BF16)
        d = per_row(dn_ref[0])
        rc_ref[...] = jnp.sum(jnp.where(lane == tok, d, 0.0), axis=1, keepdims=True)
        m_ref[...] = jnp.full(m_ref.shape, NEG_INF, F32)
        l_ref[...] = jnp.zeros_like(l_ref)
        s = _dot(qb_ref[...], kn_ref[0].astype(BF16)) + d - rc_ref[...]
        p, _ = _softmax_step(s, lane <= tok, 0, m_ref, l_ref, 1)
        acc_ref[...] = _dot_nt(p.astype(BF16), vn_ref[0].astype(BF16))

    scores = [_dot(qb_ref[...], _page_bf16(k_refs[i])) + per_row(d_ref[0, i]) - rc_ref[...] for i in range(g)]
    m_old = m_ref[0]
    m_new = functools.reduce(jnp.maximum, [jnp.max(s, axis=1, keepdims=True) for s in scores] + [m_old])
    alpha = jnp.exp(m_old - m_new)
    l_new = alpha * l_ref[0]
    acc = acc_ref[...] * alpha
    for i, s in enumerate(scores):
        p = jnp.exp(s - m_new)
        l_new = l_new + jnp.sum(p, axis=1, keepdims=True)
        acc = acc + _dot_nt(p.astype(BF16), _page_bf16(v_refs[i]))
    m_ref[0] = m_new
    l_ref[0] = l_new
    acc_ref[...] = acc

    @pl.when(j == pl.num_programs(1) - 1)
    def _():
        o_ref[0] = _heads_to_tokens(acc_ref[...] / l_ref[0])


def _page_specs(li, g):
    def page_map(i, n, j, pt):
        return (li, pt[n, j * g + i], 0, 0)

    return [pl.BlockSpec((1, 1, ATTN_HALF, PAGE_SIZE), functools.partial(page_map, i)) for i in range(g)]


def _fox_decode(page_table, li, q, k_new, v_new, d_new, d_past, cache_k, cache_v):
    db, t, _ = q.shape
    n_pages = page_table.shape[1]
    g = PAGES_PER_STEP
    sample = lambda n, j, pt: (n, 0, 0)
    new_rows = pl.BlockSpec((1, ATTN_HALF, PAGE_SIZE), sample)
    in_specs = [pl.BlockSpec((1, t, ATTN_HALF), sample), new_rows, new_rows,
                pl.BlockSpec((1, FOX_HEADS, PAGE_SIZE), sample),
                pl.BlockSpec((1, g, FOX_HEADS, PAGE_SIZE), lambda n, j, pt: (n, j, 0, 0))]
    in_specs += _page_specs(li, g) * 2
    grid_spec = pltpu.PrefetchScalarGridSpec(
        num_scalar_prefetch=1, grid=(db, n_pages // g), in_specs=in_specs,
        out_specs=pl.BlockSpec((1, t, ATTN_HALF), sample),
        scratch_shapes=[pltpu.VMEM((ROWS, ATTN_HALF), BF16), pltpu.VMEM((ROWS, 1), F32),
                        pltpu.VMEM((1, ROWS, 1), F32), pltpu.VMEM((1, ROWS, 1), F32),
                        pltpu.VMEM((ROWS, ATTN_HALF), F32)])
    return pl.pallas_call(
        functools.partial(_fox_decode_kernel, g=g),
        grid_spec=grid_spec,
        out_shape=jax.ShapeDtypeStruct((db, t, ATTN_HALF), F32),
        compiler_params=_cparams(2),
        name="fox_decode",
    )(page_table, q, k_new, v_new, d_new, d_past, *([cache_k] * g), *([cache_v] * g))


def _moba_decode_kernel(pt_ref, q_ref, qt_ref, kn_ref, vn_ref, *refs, g, n_blocks):
    k_refs, v_refs = refs[:g], refs[g:2 * g]
    o_ref, qb_ref, qft_ref, mb_ref, lb_ref, gb_ref, ob_ref, m_ref, l_ref = refs[2 * g:]
    j = pl.program_id(1)
    lane = _iota((ROWS, LANES), 1)
    row = _iota((ROWS, LANES), 0)
    tok = row // MOBA_HEADS
    pages_per_block = MOBA_BLOCK // PAGE_SIZE

    @pl.when(j == 0)
    def _():
        qb_ref[...] = (_block_diag_queries(q_ref[0]) * ATTN_SCALE).astype(BF16)
        qt = qt_ref[0]
        r_id = _iota(qt.shape, 1)
        c_id = _iota(qt.shape, 0)
        rep = jnp.zeros(qt.shape, F32)
        for t in range(ROWS // MOBA_HEADS):
            rep = jnp.where((r_id // MOBA_HEADS) == t, qt[:, t:t + 1], rep)
        qft_ref[...] = jnp.where(((c_id // HEAD_DIM) == (r_id % MOBA_HEADS)) & (r_id < ROWS), rep, 0.0)
        mb_ref[...] = jnp.zeros_like(mb_ref)
        lb_ref[...] = jnp.zeros_like(lb_ref)
        gb_ref[...] = jnp.zeros_like(gb_ref)

    blocks_per_step = g // pages_per_block
    scores = [_dot(qb_ref[...], _page_bf16(kp)) for kp in k_refs]
    mb, lb, gb = mb_ref[...], lb_ref[...], gb_ref[...]
    for bl in range(blocks_per_step):
        blk = j * blocks_per_step + bl
        pages = range(bl * pages_per_block, (bl + 1) * pages_per_block)
        kmean = jnp.sum(sum(k_refs[i][0, 0] for i in pages), axis=1, keepdims=True) * (1.0 / MOBA_BLOCK)
        gate_row = jnp.sum(qft_ref[...] * kmean, axis=0, keepdims=True)
        gate = jnp.sum(jnp.where(row == lane, gate_row, 0.0), axis=1, keepdims=True)
        m_b = functools.reduce(jnp.maximum, [jnp.max(scores[i], axis=1, keepdims=True) for i in pages])
        ps = [jnp.exp(scores[i] - m_b) for i in pages]
        l_b = sum(jnp.sum(p, axis=1, keepdims=True) for p in ps)
        ob_ref[blk] = sum(_dot_nt(p.astype(BF16), _page_bf16(v_refs[i])) for p, i in zip(ps, pages))
        here = lane == blk
        mb = jnp.where(here, m_b, mb)
        lb = jnp.where(here, l_b, lb)
        gb = jnp.where(here, gate, gb)
    mb_ref[...], lb_ref[...], gb_ref[...] = mb, lb, gb

    @pl.when(j == pl.num_programs(1) - 1)
    def _():
        m_ref[...] = jnp.full(m_ref.shape, NEG_INF, F32)
        l_ref[...] = jnp.zeros_like(l_ref)
        s_own = _dot(qb_ref[...], kn_ref[0].astype(BF16))
        p_own, _ = _softmax_step(s_own, lane <= tok, 0, m_ref, l_ref, 1)
        o_own = _dot_nt(p_own.astype(BF16), vn_ref[0].astype(BF16))
        m_own, l_own = m_ref[0], l_ref[0]
        valid = lane < n_blocks
        gate = jnp.where(valid, gb_ref[...], NEG_INF)
        rank = jnp.zeros((ROWS, LANES), jnp.int32)
        for b in range(n_blocks):
            gj = gate[:, b:b + 1]
            rank = rank + ((gj > gate) | ((gj == gate) & (lane > b))).astype(jnp.int32)
        sel = (rank < MOBA_TOPK) & valid
        m_all = jnp.maximum(jnp.max(jnp.where(sel, mb_ref[...], NEG_INF), axis=1, keepdims=True), m_own)
        wgt = jnp.where(sel, jnp.exp(jnp.where(sel, mb_ref[...] - m_all, 0.0)), 0.0)
        w_own = jnp.exp(m_own - m_all)
        den = jnp.sum(wgt * lb_ref[...], axis=1, keepdims=True) + w_own * l_own
        num = w_own * o_own
        for b in range(n_blocks):
            num = num + wgt[:, b:b + 1] * ob_ref[b]
        o_ref[0] = _heads_to_tokens(num / den)


def _moba_decode(page_table, li, q, q_t, k_new, v_new, cache_k, cache_v):
    db, t, _ = q.shape
    n_pages = page_table.shape[1]
    g = PAGES_PER_STEP
    n_blocks = n_pages * PAGE_SIZE // MOBA_BLOCK
    sample = lambda n, j, pt: (n, 0, 0)
    new_rows = pl.BlockSpec((1, ATTN_HALF, PAGE_SIZE), sample)
    tokens = pl.BlockSpec((1, t, ATTN_HALF), sample)
    grid_spec = pltpu.PrefetchScalarGridSpec(
        num_scalar_prefetch=1, grid=(db, n_pages // g),
        in_specs=[tokens, new_rows, new_rows, new_rows] + _page_specs(li, g) * 2, out_specs=tokens,
        scratch_shapes=[pltpu.VMEM((ROWS, ATTN_HALF), BF16), pltpu.VMEM((ATTN_HALF, LANES), F32),
                        pltpu.VMEM((ROWS, LANES), F32), pltpu.VMEM((ROWS, LANES), F32),
                        pltpu.VMEM((ROWS, LANES), F32), pltpu.VMEM((n_blocks, ROWS, ATTN_HALF), F32),
                        pltpu.VMEM((1, ROWS, 1), F32), pltpu.VMEM((1, ROWS, 1), F32)])
    return pl.pallas_call(
        functools.partial(_moba_decode_kernel, g=g, n_blocks=n_blocks),
        grid_spec=grid_spec,
        out_shape=jax.ShapeDtypeStruct((db, t, ATTN_HALF), F32),
        compiler_params=_cparams(2),
        name="moba_decode",
    )(page_table, q, q_t, k_new, v_new, *([cache_k] * g), *([cache_v] * g))


def _pool_kernel(u_ref, ctx_ref, w_ref, sc_ref, o_ref, ext_ref, *, tm, pos0):
    i = pl.program_id(1)

    @pl.when(i == 0)
    def _():
        ext_ref[0:POOL_HALO, :] = ctx_ref[0]

    @pl.when(i > 0)
    def _():
        ext_ref[0:POOL_HALO, :] = ext_ref[tm:tm + POOL_HALO, :]

    ext_ref[POOL_HALO:, :] = u_ref[0]
    pos = pos0 + i * tm + _iota((tm, 1), 0)
    for gi, win in enumerate(POOL_WINDOWS):
        cols = slice(gi * POOL_GROUP_WIDTH, (gi + 1) * POOL_GROUP_WIDTH)
        x = ext_ref[:, cols]
        tot = x
        shift = 1
        while shift < win:
            tot = tot + pltpu.roll(tot, shift, 0)
            shift *= 2
        cnt = jnp.minimum(win, pos + 1).astype(F32)
        d = tot[POOL_HALO:] / cnt - x[POOL_HALO:]
        y = _dot(d.astype(BF16), w_ref[gi])
        o_ref[0, :, cols] = y * sc_ref[:, cols]


def _pool_mix(u, ctx, w_pool, pool_scale, pos0):
    n, l, _ = u.shape
    tm = min(512, l)
    return pl.pallas_call(
        functools.partial(_pool_kernel, tm=tm, pos0=pos0),
        grid=(n, l // tm),
        in_specs=[pl.BlockSpec((1, tm, POOL_WIDTH), lambda b, i: (b, i, 0)),
                  pl.BlockSpec((1, POOL_HALO, POOL_WIDTH), lambda b, i: (b, 0, 0)),
                  pl.BlockSpec((len(POOL_WINDOWS), POOL_GROUP_WIDTH, POOL_GROUP_WIDTH), lambda b, i: (0, 0, 0)),
                  pl.BlockSpec((1, POOL_WIDTH), lambda b, i: (0, 0))],
        out_specs=pl.BlockSpec((1, tm, POOL_WIDTH), lambda b, i: (b, i, 0)),
        out_shape=jax.ShapeDtypeStruct((n, l, POOL_WIDTH), F32),
        scratch_shapes=[pltpu.VMEM((POOL_HALO + tm, POOL_WIDTH), F32)],
        compiler_params=_cparams(2),
        name="pool_mix",
    )(u, ctx, w_pool, pool_scale)


def _gdn_prep_kernel(x_ref, ctx_ref, ab_ref, cw_ref, alog_ref, dtb_ref, q_ref, k_ref, v_ref, gb_ref, ext_ref, *, tm):
    i = pl.program_id(1)

    @pl.when(i == 0)
    def _():
        ext_ref[0:CONV_HALO, :] = ctx_ref[0]

    @pl.when(i > 0)
    def _():
        ext_ref[0:CONV_HALO, :] = ext_ref[tm:tm + CONV_HALO, :]

    ext_ref[CONV_HALO:, :] = x_ref[0]
    for part, out_ref in enumerate((q_ref, k_ref, v_ref)):
        for h in range(GDN_HEADS):
            c0 = part * GDN_WIDTH + h * GDN_HEAD_DIM
            x = ext_ref[:, c0:c0 + GDN_HEAD_DIM]
            y = x[CONV_HALO:] * cw_ref[CONV_WIDTH - 1:CONV_WIDTH, c0:c0 + GDN_HEAD_DIM]
            for tap in range(CONV_WIDTH - 1):
                shifted = pltpu.roll(x, CONV_CTX - tap, 0)[CONV_HALO:]
                y = y + shifted * cw_ref[tap:tap + 1, c0:c0 + GDN_HEAD_DIM]
            y = y * _sigmoid(y)
            if part < 2:
                y = y * lax.rsqrt(jnp.sum(y * y, axis=-1, keepdims=True) + NORM_EPS)
            if part == 0:
                y = y * GDN_SCALE
            out_ref[0, :, h * GDN_HEAD_DIM:(h + 1) * GDN_HEAD_DIM] = y
    ab = ab_ref[0]
    g = -jnp.exp(alog_ref[...]) * _softplus(ab + dtb_ref[...])
    beta = _sigmoid(ab)
    gb_ref[0] = jnp.where(_iota(ab.shape, 1) < GDN_HEADS, g, beta)


def _gdn_prep(qkv, ctx, ab, conv_w, a_log_row, dt_bias_row):
    n, l, width = qkv.shape
    tm = min(256, l)
    outs = [jax.ShapeDtypeStruct((n, l, GDN_WIDTH), F32)] * 3 + [jax.ShapeDtypeStruct((n, l, LANES), F32)]
    row = lambda b, i: (b, i, 0)
    fixed = lambda b, i: (0, 0)
    return pl.pallas_call(
        functools.partial(_gdn_prep_kernel, tm=tm),
        grid=(n, l // tm),
        in_specs=[pl.BlockSpec((1, tm, width), row),
                  pl.BlockSpec((1, CONV_HALO, width), lambda b, i: (b, 0, 0)),
                  pl.BlockSpec((1, tm, LANES), row),
                  pl.BlockSpec((CONV_WIDTH, width), fixed),
                  pl.BlockSpec((1, LANES), fixed),
                  pl.BlockSpec((1, LANES), fixed)],
        out_specs=[pl.BlockSpec((1, tm, GDN_WIDTH), row)] * 3 + [pl.BlockSpec((1, tm, LANES), row)],
        out_shape=outs,
        scratch_shapes=[pltpu.VMEM((CONV_HALO + tm, width), F32)],
        compiler_params=_cparams(2),
        name="gdn_prep",
    )(qkv, ctx, ab, conv_w, a_log_row, dt_bias_row)


def _gdn_chunk_kernel(q_ref, k_ref, v_ref, gb_ref, u_ref, w_ref, qg_ref, kd_ref, at_ref, eg_ref, *, c, chunks):
    ri = _iota((c, c), 0)
    ci = _iota((c, c), 1)
    eye = ri == ci
    incl = ri >= ci
    strict = ri > ci
    eye_f = eye.astype(F32)
    levels = max(1, (c - 1).bit_length())
    pairs = [(ch, h) for ch in range(chunks) for h in range(GDN_HEADS)]
    rows = lambda ch: slice(ch * c, (ch + 1) * c)
    cols = lambda h: slice(h * GDN_HEAD_DIM, (h + 1) * GDN_HEAD_DIM)
    ks = [k_ref[0, rows(ch), cols(h)] for ch, h in pairs]
    g_cols = [gb_ref[0, rows(ch), h:h + 1] for ch, h in pairs]
    betas = [gb_ref[0, rows(ch), GDN_HEADS + h:GDN_HEADS + h + 1] for ch, h in pairs]
    gc_cols, decays = [], []
    for g_col in g_cols:
        g_row = jnp.sum(jnp.where(eye, g_col, 0.0), axis=0, keepdims=True)
        gc_col = jnp.sum(jnp.where(incl, g_row, 0.0), axis=1, keepdims=True)
        gc_row = jnp.sum(jnp.where(ri <= ci, g_col, 0.0), axis=0, keepdims=True)
        gc_cols.append(gc_col)
        decays.append(jnp.where(incl, jnp.exp(jnp.where(incl, gc_col - gc_row, 0.0)), 0.0))
    kbetas = [k * beta for k, beta in zip(ks, betas)]
    a_mats = [jnp.where(strict, _mm_nt(kb, k) * d, 0.0) for kb, k, d in zip(kbetas, ks, decays)]
    ts = [eye_f - a for a in a_mats]
    npows = [_mm(a, a) for a in a_mats]
    for lvl in range(1, levels):
        ts = [t + _mm(t, npow) for t, npow in zip(ts, npows)]
        if lvl < levels - 1:
            npows = [_mm(npow, npow) for npow in npows]
    egs = [jnp.exp(gc) for gc in gc_cols]
    for idx, (ch, h) in enumerate(pairs):
        r, cl = rows(ch), cols(h)
        q, k, gc = q_ref[0, r, cl], ks[idx], gc_cols[idx]
        u_ref[0, r, cl] = _mm(ts[idx], v_ref[0, r, cl] * betas[idx])
        w_ref[0, r, cl] = _mm(ts[idx], kbetas[idx] * egs[idx])
        qg_ref[0, r, cl] = q * egs[idx]
        kd_ref[0, r, cl] = k * jnp.exp(gc[c - 1:c, :] - gc)
        at_ref[0, r, h * c:(h + 1) * c] = jnp.where(incl, _mm_nt(q, k) * decays[idx], 0.0)
    lane = _iota((c, LANES), 1)
    for ch in range(chunks):
        eg_full = jnp.zeros((c, LANES), F32)
        for h in range(GDN_HEADS):
            eg_full = jnp.where(lane == h, egs[ch * GDN_HEADS + h], eg_full)
        eg_ref[0, rows(ch), :] = eg_full


def _gdn_chunks(q, k, v, gb, c):
    n, l, _ = q.shape
    chunks = max(1, min(4, l // c))
    tm = chunks * c
    row = lambda b, i: (b, i, 0)
    big = jax.ShapeDtypeStruct((n, l, GDN_WIDTH), F32)
    outs = [big] * 4 + [jax.ShapeDtypeStruct((n, l, GDN_HEADS * c), F32), jax.ShapeDtypeStruct((n, l, LANES), F32)]
    return pl.pallas_call(
        functools.partial(_gdn_chunk_kernel, c=c, chunks=chunks),
        grid=(n, l // tm),
        in_specs=[pl.BlockSpec((1, tm, GDN_WIDTH), row)] * 3 + [pl.BlockSpec((1, tm, LANES), row)],
        out_specs=[pl.BlockSpec((1, tm, GDN_WIDTH), row)] * 4
        + [pl.BlockSpec((1, tm, GDN_HEADS * c), row), pl.BlockSpec((1, tm, LANES), row)],
        out_shape=outs,
        compiler_params=_cparams(2),
        name="gdn_chunks",
    )(q, k, v, gb)


SCAN_BATCH = 4


def _gdn_scan_kernel(u_ref, w_ref, qg_ref, kd_ref, at_ref, eg_ref, z_ref, s0_ref, nw_ref, o_ref, sout_ref, s_ref, *, c):
    i = pl.program_id(1)

    @pl.when(i == 0)
    def _():
        s_ref[...] = s0_ref[...]

    pairs = [(b, h) for b in range(s_ref.shape[0]) for h in range(GDN_HEADS)]
    cols = lambda h: slice(h * GDN_HEAD_DIM, (h + 1) * GDN_HEAD_DIM)
    states = [s_ref[b, h] for b, h in pairs]
    v_news = [u_ref[b, :, cols(h)] - _mm(w_ref[b, :, cols(h)], s) for (b, h), s in zip(pairs, states)]
    outs = [_mm(qg_ref[b, :, cols(h)], s) + _mm(at_ref[b, :, h * c:(h + 1) * c], v)
            for (b, h), s, v in zip(pairs, states, v_news)]
    for (b, h), s, v, o in zip(pairs, states, v_news, outs):
        s_ref[b, h] = s * eg_ref[b, c - 1:c, h:h + 1] + _mm_tn(kd_ref[b, :, cols(h)], v)
        o = o * lax.rsqrt(jnp.mean(o * o, axis=-1, keepdims=True) + NORM_EPS) * nw_ref[...]
        z = z_ref[b, :, cols(h)]
        o_ref[b, :, cols(h)] = o * (z * _sigmoid(z))

    @pl.when(i == pl.num_programs(1) - 1)
    def _():
        sout_ref[...] = s_ref[...]


def _gdn_scan(u, w, qg, kd, at, eg, z, s0, norm_w, c):
    n, l, _ = u.shape
    nb = SCAN_BATCH if n % SCAN_BATCH == 0 else 1
    row = lambda b, i: (b, i, 0)
    state = lambda b, i: (b, 0, 0, 0)
    st_shape = (nb, GDN_HEADS, GDN_HEAD_DIM, GDN_HEAD_DIM)
    return pl.pallas_call(
        functools.partial(_gdn_scan_kernel, c=c),
        grid=(n // nb, l // c),
        in_specs=[pl.BlockSpec((nb, c, GDN_WIDTH), row)] * 4
        + [pl.BlockSpec((nb, c, GDN_HEADS * c), row), pl.BlockSpec((nb, c, LANES), row),
           pl.BlockSpec((nb, c, GDN_WIDTH), row), pl.BlockSpec(st_shape, state),
           pl.BlockSpec((1, GDN_HEAD_DIM), lambda b, i: (0, 0))],
        out_specs=[pl.BlockSpec((nb, c, GDN_WIDTH), row), pl.BlockSpec(st_shape, state)],
        out_shape=[jax.ShapeDtypeStruct((n, l, GDN_WIDTH), F32),
                   jax.ShapeDtypeStruct((n,) + st_shape[1:], F32)],
        scratch_shapes=[pltpu.VMEM(st_shape, F32)],
        compiler_params=_cparams(2),
        name="gdn_scan",
    )(u, w, qg, kd, at, eg, z, s0, norm_w)


def _pad_lanes(x, width=LANES):
    return jnp.pad(x, [(0, 0)] * (x.ndim - 1) + [(0, width - x.shape[-1])])


def _even_weights(w_in, b_f):
    fq, fk, fv, fl, mq, mk, mv = jnp.split(
        w_in, [ATTN_HALF, 2 * ATTN_HALF, 3 * ATTN_HALF, 3 * ATTN_HALF + FOX_HEADS,
               4 * ATTN_HALF + FOX_HEADS, 5 * ATTN_HALF + FOX_HEADS], axis=1)
    w_main = jnp.concatenate([fq, fk, fv, mq, mk, mv], axis=1).astype(BF16)
    return w_main, _pad_lanes(fl).astype(BF16), _pad_lanes(b_f[None, :])


def _odd_weights(w_in):
    u, qkv, a, b, z = jnp.split(
        w_in, [POOL_WIDTH, POOL_WIDTH + 3 * GDN_WIDTH, POOL_WIDTH + 3 * GDN_WIDTH + GDN_HEADS,
               POOL_WIDTH + 3 * GDN_WIDTH + 2 * GDN_HEADS], axis=1)
    w_main = jnp.concatenate([u, qkv, z], axis=1).astype(BF16)
    return w_main, _pad_lanes(jnp.concatenate([a, b], axis=1)).astype(BF16)


def _odd_mixer(h, n, l, pool_ctx, conv_ctx, s0, pos0, chunk, w_main, w_small, conv_w, a_log_row, dt_bias_row,
               norm_w, w_pool, pool_scale):
    u, qkv, z, ab = _proj(h, w_main, w_small, jnp.zeros((1, LANES), F32),
                          (POOL_WIDTH, 3 * GDN_WIDTH, GDN_WIDTH), False)
    u3, qkv3, z3, ab3 = (t.reshape(n, l, -1) for t in (u, qkv, z, ab))
    lp = -(-l // SUBLANES) * SUBLANES
    padr = lambda t: jnp.pad(t, ((0, 0), (0, lp - l), (0, 0)))
    ctx_p = jnp.pad(pool_ctx, ((0, 0), (POOL_HALO - POOL_CTX, 0), (0, 0)))
    ctx_c = jnp.pad(conv_ctx, ((0, 0), (CONV_HALO - CONV_CTX, 0), (0, 0)))
    y_pool = _pool_mix(padr(u3), ctx_p, w_pool, pool_scale, pos0)[:, :l]
    q, k, v, gb = _gdn_prep(padr(qkv3), ctx_c, padr(ab3), conv_w, a_log_row, dt_bias_row)
    if lp != l:
        gb = gb * (jnp.arange(lp) < l).astype(F32)[None, :, None]
    if chunk % SUBLANES:
        assert chunk == l
        chunk = lp
    c = chunk
    uu, ww, qg, kd, at, eg = _gdn_chunks(q, k, v, gb, c)
    o, s_new = _gdn_scan(uu, ww, qg, kd, at, eg, padr(z3), s0, norm_w, c)
    new_pool = jnp.concatenate([pool_ctx, u3], axis=1)[:, -POOL_CTX:]
    new_conv = jnp.concatenate([conv_ctx, qkv3], axis=1)[:, -CONV_CTX:]
    return (y_pool.reshape(n * l, POOL_WIDTH), o[:, :l].reshape(n * l, GDN_WIDTH), new_pool, new_conv, s_new)


def kernel(x_prompt, x_sample, cache_fox_k, cache_fox_v, cache_fox_logf, cache_moba_k, cache_moba_v, state_pool, state_conv, state_gdn, page_table, w_in_even, b_forget, w_out_even, w_in_odd, conv_w, a_log, dt_bias, gdn_norm_w, w_pool, pool_scale, w_out_odd, ln1_g, ln1_b, ln2_g, ln2_b, w_gate_up, w_down):
    B, S, _ = x_prompt.shape
    DB, T, _ = x_sample.shape
    P = page_table.shape[1] * PAGE_SIZE
    hp = x_prompt.reshape(B * S, D_MODEL)
    hs = x_sample.reshape(DB * T, D_MODEL)

    assert T * FOX_HEADS == ROWS and P % MOBA_BLOCK == 0
    cfl_t = jnp.swapaxes(cache_fox_logf, 2, 3)
    paged_t = lambda c: jnp.transpose(c, (0, 1, 3, 4, 2)).reshape(c.shape[0], c.shape[1], ATTN_HALF, PAGE_SIZE)
    cfk, cfv, cmk, cmv = paged_t(cache_fox_k), paged_t(cache_fox_v), paged_t(cache_moba_k), paged_t(cache_moba_v)
    new_t = lambda t: jnp.pad(jnp.swapaxes(t.reshape(DB, T, ATTN_HALF), 1, 2), ((0, 0), (0, 0), (0, PAGE_SIZE - T)))

    outs_p = {k: [] for k in ("fk", "fv", "fl", "mk", "mv", "pool", "conv", "gdn")}
    outs_s = {k: [] for k in ("fk", "fv", "fl", "mk", "mv", "pool", "conv", "gdn")}
    widths_even = (ATTN_HALF,) * 6
    n_even = cache_fox_k.shape[0]
    kv_t = ()
    for layer in range(DEPTH):
        li = layer // 2
        row = lambda a: a[layer][None, :]
        if layer % 2 == 0:
            w_main, w_fl, b_fl = _even_weights(w_in_even[li], b_forget[li])
            w_out = w_out_even[li].astype(BF16)
            fq, fk, fv, mq, mk, mv, fl = _proj(hp, w_main, w_fl, b_fl, widths_even, True,
                                               transposed=(False, True, True, False, True, True), batch=B,
                                               layer=(li, n_even), carried=kv_t)
            kv_t = (fk, fv, mk, mv)
            b3 = lambda t: t.reshape(B, S, ATTN_HALF)
            logf = fl[:, :FOX_HEADS].reshape(B, S, FOX_HEADS)
            c_rows = _cumsum_rows(jnp.swapaxes(logf, 1, 2)).reshape(B, FOX_HEADS // 2, 2, S)
            fox = _prompt_attn(b3(fq), fk, fv, c_rows, False, li)
            moba = _prompt_attn(b3(mq), mk, mv, c_rows, True, li)
            a1_p, a2_p = fox.reshape(B * S, ATTN_HALF), moba.reshape(B * S, ATTN_HALF)
            hd = lambda t, n, l: t.reshape(n, l, FOX_HEADS, HEAD_DIM)
            outs_p["fl"].append(logf)
            fq, fk, fv, mq, mk, mv, fl = _proj(hs, w_main, w_fl, b_fl, widths_even, True)
            logf = fl[:, :FOX_HEADS].reshape(DB, T, FOX_HEADS)
            l_new = _pad_lanes(jnp.swapaxes(logf, 1, 2), PAGE_SIZE)
            d_new, d_past = _decay(page_table, li, l_new, cfl_t)
            tok = lambda t: t.reshape(DB, T, ATTN_HALF)
            fox = _fox_decode(page_table, li, tok(fq), new_t(fk), new_t(fv), d_new, d_past, cfk, cfv)
            moba = _moba_decode(page_table, li, tok(mq), new_t(mq), new_t(mk), new_t(mv), cmk, cmv)
            a1_s, a2_s = fox.reshape(DB * T, ATTN_HALF), moba.reshape(DB * T, ATTN_HALF)
            for key, val in (("fk", hd(fk, DB, T)), ("fv", hd(fv, DB, T)), ("fl", logf),
                             ("mk", hd(mk, DB, T)), ("mv", hd(mv, DB, T))):
                outs_s[key].append(val)
            w1, w2 = w_out[:ATTN_HALF], w_out[ATTN_HALF:]
        else:
            w_main, w_small = _odd_weights(w_in_odd[li])
            w_out = w_out_odd[li].astype(BF16)
            shared = (w_main, w_small, conv_w[li], _pad_lanes(a_log[li][None, :]), _pad_lanes(dt_bias[li][None, :]),
                      gdn_norm_w[li][None, :], w_pool[li].astype(BF16), pool_scale[li][None, :])
            a1_p, a2_p, sp, sc, sg = _odd_mixer(
                hp, B, S, jnp.zeros((B, POOL_CTX, POOL_WIDTH), F32), jnp.zeros((B, CONV_CTX, 3 * GDN_WIDTH), F32),
                jnp.zeros((B, GDN_HEADS, GDN_HEAD_DIM, GDN_HEAD_DIM), F32), 0, GDN_CHUNK, *shared)
            outs_p["pool"].append(sp); outs_p["conv"].append(sc); outs_p["gdn"].append(sg)
            a1_s, a2_s, sp, sc, sg = _odd_mixer(
                hs, DB, T, state_pool[li], state_conv[li], state_gdn[li], P, T, *shared)
            outs_s["pool"].append(sp); outs_s["conv"].append(sc); outs_s["gdn"].append(sg)
            w1, w2 = w_out[:POOL_WIDTH], w_out[POOL_WIDTH:]
        wgu, wdn = w_gate_up[layer].astype(BF16), w_down[layer].astype(BF16)
        hp = _outproj_ln(a1_p, a2_p, hp, w1, w2, row(ln1_g), row(ln1_b))
        hp = _ffn_ln(hp, wgu, wdn, row(ln2_g), row(ln2_b))
        hs = _outproj_ln(a1_s, a2_s, hs, w1, w2, row(ln1_g), row(ln1_b))
        hs = _ffn_ln(hs, wgu, wdn, row(ln2_g), row(ln2_b))

    st = jnp.stack
    keys = ("fk", "fv", "fl", "mk", "mv", "pool", "conv", "gdn")
    heads_last = lambda t: jnp.transpose(t.reshape(n_even, B, FOX_HEADS, HEAD_DIM, S), (0, 1, 4, 2, 3))
    prompt_kv = dict(zip(("fk", "fv", "mk", "mv"), (heads_last(t) for t in kv_t)))
    return ((hp.reshape(B, S, D_MODEL), hs.reshape(DB, T, D_MODEL))
            + tuple(prompt_kv[k] if k in prompt_kv else st(outs_p[k]) for k in keys)
            + tuple(st(outs_s[k]) for k in keys))
```

```python
import functools

import jax
import jax.numpy as jnp
from jax import lax
from jax.experimental import pallas as pl
from jax.experimental.pallas import tpu as pltpu

F32 = jnp.float32
BF16 = jnp.bfloat16
HI = lax.Precision.HIGHEST

D_MODEL = 1024
HEAD_DIM = 64
FOX_HEADS = 8
MOBA_HEADS = 8
ATTN_HALF = FOX_HEADS * HEAD_DIM
ATTN_SCALE = HEAD_DIM ** -0.5
MOBA_BLOCK = 256
MOBA_TOPK = 3
PAGE_SIZE = 128
POOL_WINDOWS = (2, 4, 8, 16)
POOL_WIDTH = 512
POOL_GROUP_WIDTH = 128
POOL_CTX = 15
GDN_HEADS = 4
GDN_HEAD_DIM = 128
GDN_WIDTH = GDN_HEADS * GDN_HEAD_DIM
GDN_SCALE = GDN_HEAD_DIM ** -0.5
GDN_CHUNK = 64
CONV_WIDTH = 4
CONV_CTX = CONV_WIDTH - 1
DEPTH = 4
DEEPNORM_ALPHA = (2 * DEPTH) ** 0.25
LN_EPS = 1e-5
NORM_EPS = 1e-6
NEG_INF = -1e30

LANES = 128
SUBLANES = 8
POOL_HALO = 16
CONV_HALO = 8
VMEM_LIMIT = 56 * 1024 * 1024


def _cparams(n_axes):
    return pltpu.CompilerParams(dimension_semantics=("arbitrary",) * n_axes,
                                vmem_limit_bytes=VMEM_LIMIT)


def _dot(a, b, precision=lax.Precision.DEFAULT):
    return jnp.dot(a, b, preferred_element_type=F32, precision=precision)


def _dot_split3(a, b):
    hi = a.astype(BF16)
    mid = (a - hi.astype(F32)).astype(BF16)
    lo = (a - hi.astype(F32) - mid.astype(F32)).astype(BF16)
    return _dot(hi, b) + _dot(mid, b) + _dot(lo, b)


def _dot_nt(a, b, precision=lax.Precision.DEFAULT):
    return lax.dot_general(a, b, (((1,), (1,)), ((), ())), preferred_element_type=F32, precision=precision)


def _split2(a):
    hi = a.astype(BF16)
    return hi, (a - hi.astype(F32)).astype(BF16)


def _mm3(a, b, dims):
    ah, al = _split2(a)
    bh, bl = _split2(b)
    dg = lambda x, y: lax.dot_general(x, y, dims, preferred_element_type=F32)
    return dg(ah, bh) + dg(ah, bl) + dg(al, bh)


def _mm(a, b):
    return _mm3(a, b, (((1,), (0,)), ((), ())))


def _mm_nt(a, b):
    return _mm3(a, b, (((1,), (1,)), ((), ())))


def _mm_tn(a, b):
    return _mm3(a, b, (((0,), (0,)), ((), ())))


def _sigmoid(x):
    return 1.0 / (1.0 + jnp.exp(-x))


def _softplus(x):
    return jnp.maximum(x, 0.0) + jnp.log1p(jnp.exp(-jnp.abs(x)))


def _layer_norm(y, g, b):
    mu = jnp.mean(y, axis=-1, keepdims=True)
    yc = y - mu
    var = jnp.mean(yc * yc, axis=-1, keepdims=True)
    return yc * lax.rsqrt(var + LN_EPS) * g + b


def _iota(shape, axis):
    return lax.broadcasted_iota(jnp.int32, shape, axis)


def _proj_kernel(x_ref, w_ref, wt_ref, ws_ref, bs_ref, *refs, widths, transposed, log_sigmoid, n_carried):
    out_refs = refs[n_carried:]
    xb = x_ref[...].astype(BF16)
    off = 0
    for o_ref, wd, tr in zip(out_refs[:-1], widths, transposed):
        if tr:
            o_ref[0, 0] = _dot_nt(wt_ref[off:off + wd, :], xb)
        else:
            o_ref[...] = _dot(xb, w_ref[:, off:off + wd])
        off += wd
    small = _dot(xb, ws_ref[...]) + bs_ref[...]
    if log_sigmoid:
        small = -_softplus(-small)
    out_refs[-1][...] = small


def _proj(x, w_main, w_small, b_small, widths, log_sigmoid, transposed=None, batch=1, layer=(0, 1), carried=()):
    m = x.shape[0]
    tm = min(512, m)
    n_main = w_main.shape[1]
    transposed = transposed or (False,) * len(widths)
    li, n_layers = layer
    seq = m // batch
    per_batch = seq // tm
    w_t = w_main.T if any(transposed) else jnp.zeros((SUBLANES, D_MODEL), BF16)
    outs, out_specs, t_slots = [], [], []
    for wd, tr in zip(widths, transposed):
        if tr:
            t_slots.append(len(outs))
            outs.append(jax.ShapeDtypeStruct((n_layers, batch, wd, seq), F32))
            out_specs.append(pl.BlockSpec((1, 1, wd, tm), lambda i: (li, i // per_batch, 0, i % per_batch)))
        else:
            outs.append(jax.ShapeDtypeStruct((m, wd), F32))
            out_specs.append(pl.BlockSpec((tm, wd), lambda i: (i, 0)))
    outs.append(jax.ShapeDtypeStruct((m, LANES), F32))
    out_specs.append(pl.BlockSpec((tm, LANES), lambda i: (i, 0)))
    n_fixed = 5
    return pl.pallas_call(
        functools.partial(_proj_kernel, widths=widths, transposed=transposed, log_sigmoid=log_sigmoid,
                          n_carried=len(carried)),
        grid=(m // tm,),
        in_specs=[pl.BlockSpec((tm, D_MODEL), lambda i: (i, 0)),
                  pl.BlockSpec((D_MODEL, n_main), lambda i: (0, 0)),
                  pl.BlockSpec(w_t.shape, lambda i: (0, 0)),
                  pl.BlockSpec((D_MODEL, LANES), lambda i: (0, 0)),
                  pl.BlockSpec((1, LANES), lambda i: (0, 0))]
        + [pl.BlockSpec(memory_space=pl.ANY)] * len(carried),
        out_specs=out_specs,
        out_shape=outs,
        input_output_aliases={n_fixed + k: t_slots[k] for k in range(len(carried))},
        compiler_params=_cparams(1),
        name="in_proj",
    )(x, w_main, w_t, w_small, b_small, *carried)


def _outproj_ln_kernel(a1_ref, a2_ref, x_ref, w1_ref, w2_ref, g_ref, b_ref, o_ref):
    mix = _dot(a1_ref[...].astype(BF16), w1_ref[...]) + _dot(a2_ref[...].astype(BF16), w2_ref[...])
    o_ref[...] = _layer_norm(DEEPNORM_ALPHA * x_ref[...] + mix, g_ref[...], b_ref[...])


def _outproj_ln(a1, a2, x, w1, w2, g, b):
    m = x.shape[0]
    tm = min(512, m)
    k1, k2 = a1.shape[1], a2.shape[1]
    return pl.pallas_call(
        _outproj_ln_kernel,
        grid=(m // tm,),
        in_specs=[pl.BlockSpec((tm, k1), lambda i: (i, 0)),
                  pl.BlockSpec((tm, k2), lambda i: (i, 0)),
                  pl.BlockSpec((tm, D_MODEL), lambda i: (i, 0)),
                  pl.BlockSpec((k1, D_MODEL), lambda i: (0, 0)),
                  pl.BlockSpec((k2, D_MODEL), lambda i: (0, 0)),
                  pl.BlockSpec((1, D_MODEL), lambda i: (0, 0)),
                  pl.BlockSpec((1, D_MODEL), lambda i: (0, 0))],
        out_specs=pl.BlockSpec((tm, D_MODEL), lambda i: (i, 0)),
        out_shape=jax.ShapeDtypeStruct((m, D_MODEL), F32),
        compiler_params=_cparams(1),
        name="outproj_ln",
    )(a1, a2, x, w1, w2, g, b)


def _ffn_kernel(x_ref, wg_ref, wu_ref, wd_ref, g_ref, b_ref, o_ref, xb_ref, acc_ref):
    j = pl.program_id(1)

    @pl.when(j == 0)
    def _():
        xb_ref[...] = x_ref[...].astype(BF16)
        acc_ref[...] = jnp.zeros_like(acc_ref)

    xb = xb_ref[...]
    gate = _dot(xb, wg_ref[...])
    up = _dot(xb, wu_ref[...])
    act = (gate * _sigmoid(gate) * up).astype(BF16)
    acc_ref[...] += _dot(act, wd_ref[...])

    @pl.when(j == pl.num_programs(1) - 1)
    def _():
        o_ref[...] = _layer_norm(DEEPNORM_ALPHA * x_ref[...] + acc_ref[...], g_ref[...], b_ref[...])


def _ffn_ln(x, w_gate_up, w_down, g, b):
    m = x.shape[0]
    d_ff = w_down.shape[0]
    tm = min(256, m)
    tf = d_ff
    nf = d_ff // tf
    return pl.pallas_call(
        _ffn_kernel,
        grid=(m // tm, nf),
        in_specs=[pl.BlockSpec((tm, D_MODEL), lambda i, j: (i, 0)),
                  pl.BlockSpec((D_MODEL, tf), lambda i, j: (0, j)),
                  pl.BlockSpec((D_MODEL, tf), lambda i, j: (0, j + nf)),
                  pl.BlockSpec((tf, D_MODEL), lambda i, j: (j, 0)),
                  pl.BlockSpec((1, D_MODEL), lambda i, j: (0, 0)),
                  pl.BlockSpec((1, D_MODEL), lambda i, j: (0, 0))],
        out_specs=pl.BlockSpec((tm, D_MODEL), lambda i, j: (i, 0)),
        out_shape=jax.ShapeDtypeStruct((m, D_MODEL), F32),
        scratch_shapes=[pltpu.VMEM((tm, D_MODEL), BF16), pltpu.VMEM((tm, D_MODEL), F32)],
        compiler_params=_cparams(2),
        name="ffn_ln",
    )(x, w_gate_up, w_gate_up, w_down, g, b)


def _cumsum_kernel(x_ref, o_ref, *, blk):
    n = x_ref.shape[-1]
    tri = (_iota((blk, blk), 0) <= _iota((blk, blk), 1)).astype(F32)
    carry = jnp.zeros((x_ref.shape[1], 1), F32)
    for i in range(n // blk):
        y = _dot(x_ref[0, :, i * blk:(i + 1) * blk], tri, HI) + carry
        o_ref[0, :, i * blk:(i + 1) * blk] = y
        carry = y[:, blk - 1:blk]


def _cumsum_rows(x):
    b, h, s = x.shape
    return pl.pallas_call(
        functools.partial(_cumsum_kernel, blk=min(256, s)),
        grid=(b,),
        in_specs=[pl.BlockSpec((1, h, s), lambda i: (i, 0, 0))],
        out_specs=pl.BlockSpec((1, h, s), lambda i: (i, 0, 0)),
        out_shape=jax.ShapeDtypeStruct((b, h, s), F32),
        compiler_params=_cparams(1),
        name="logf_cumsum",
    )(x)


KEY_TILES_PER_UPDATE = 4
LOG2E = 1.4426950408889634


def _softmax_step(s, valid, h, m_ref, l_ref, axis):
    if valid is not None:
        s = jnp.where(valid, s, NEG_INF)
    m_old = m_ref[h]
    m_new = jnp.maximum(m_old, jnp.max(s, axis=axis, keepdims=True))
    alpha = jnp.exp(m_old - m_new)
    p = jnp.exp(s - m_new)
    if valid is not None:
        p = jnp.where(valid, p, 0.0)
    l_ref[h] = alpha * l_ref[h] + jnp.sum(p, axis=axis, keepdims=True)
    m_ref[h] = m_new
    return p, alpha


def _prompt_attn_kernel(q_ref, k_ref, v_ref, c_ref, o_ref, kb_ref, vt_ref, aux_ref, sel_ref,
                        m_ref, l_ref, acc_ref, *, t, moba):
    qi = pl.program_id(2)
    seq = k_ref.shape[3]
    nblk = seq // t
    krow = _iota((t, t), 0)
    qcol = _iota((t, t), 1)

    @pl.when(qi == 0)
    def _():
        vt_ref[...] = v_ref[0, 0].astype(BF16)
        if moba:
            aux_ref[...] = jnp.zeros_like(aux_ref)
        for blk in range(nblk):
            rows = slice(blk * t, (blk + 1) * t)
            k_blk = k_ref[0, 0, :, rows].T
            kb_ref[rows, :] = k_blk.astype(BF16)
            if moba:
                aux_ref[blk:blk + 1, :] = jnp.sum(k_blk, axis=0, keepdims=True) * (1.0 / MOBA_BLOCK)
            else:
                for h in range(2):
                    c_col = jnp.sum(jnp.where(krow == qcol, c_ref[0, 0, h:h + 1, rows], 0.0), axis=1, keepdims=True)
                    aux_ref[h, rows, :] = jnp.broadcast_to(c_col * LOG2E, (t, LANES))

    q = q_ref[0]
    head0 = _iota((t, LANES), 1) < HEAD_DIM
    q_heads = (jnp.where(head0, q, 0.0), jnp.where(head0, 0.0, q))
    qb = tuple((qh * (ATTN_SCALE * LOG2E)).astype(BF16) for qh in q_heads)

    if moba:
        nbp = aux_ref.shape[0]
        blk_id = _iota((nbp, t), 0)
        for h in range(2):
            gate = _dot_nt(aux_ref[...], q_heads[h], HI)
            gate = jnp.where(blk_id < qi, gate, NEG_INF)
            rank = jnp.zeros((nbp, t), jnp.int32)
            for j in range(nblk):
                gj = gate[j:j + 1, :]
                rank = rank + ((gj > gate) | ((gj == gate) & (blk_id > j))).astype(jnp.int32)
            sel = jnp.where((rank < MOBA_TOPK) & (blk_id < qi), 1.0, 0.0)
            for j in range(nblk):
                sel_ref[h, j] = sel[j:j + 1, :]

    m_ref[...] = jnp.full(m_ref.shape, NEG_INF, F32)
    l_ref[...] = jnp.zeros_like(l_ref)
    acc_ref[...] = jnp.zeros_like(acc_ref)

    def tiles(ki0, count, diagonal_last):
        starts = [pl.multiple_of((ki0 + i) * t, t) for i in range(count)]
        for h in range(2):
            scores, valids = [], []
            for i, k0 in enumerate(starts):
                s = _dot_nt(kb_ref[pl.ds(k0, t), :], qb[h])
                diagonal = diagonal_last and i == count - 1
                if moba:
                    valid = (krow <= qcol) if diagonal else (sel_ref[h, ki0 + i] > 0.5)
                else:
                    ck = aux_ref[h, pl.ds(k0, t), :]
                    s = s - jnp.concatenate([ck] * (t // LANES), axis=1)
                    valid = (krow <= qcol) if diagonal else None
                scores.append(s if valid is None else jnp.where(valid, s, NEG_INF))
                valids.append(valid)
            m_old = m_ref[h]
            m_new = functools.reduce(jnp.maximum, [jnp.max(s, axis=0, keepdims=True) for s in scores] + [m_old])
            alpha = jnp.exp2(m_old - m_new)
            rows = slice(h * HEAD_DIM, (h + 1) * HEAD_DIM)
            l_new = alpha * l_ref[h]
            acc = acc_ref[rows, :] * alpha
            for s, valid, k0 in zip(scores, valids, starts):
                p = jnp.exp2(s - m_new)
                if valid is not None:
                    p = jnp.where(valid, p, 0.0)
                l_new = l_new + jnp.sum(p, axis=0, keepdims=True)
                acc = acc + _dot(vt_ref[rows, pl.ds(k0, t)], p.astype(BF16))
            m_ref[h] = m_new
            l_ref[h] = l_new
            acc_ref[rows, :] = acc

    def body(gi, carry):
        tiles(gi * KEY_TILES_PER_UPDATE, KEY_TILES_PER_UPDATE, False)
        return carry

    full = qi // KEY_TILES_PER_UPDATE
    lax.fori_loop(0, full, body, 0)
    for rest in range(KEY_TILES_PER_UPDATE):
        @pl.when(qi - full * KEY_TILES_PER_UPDATE == rest)
        def _():
            tiles(full * KEY_TILES_PER_UPDATE, rest + 1, True)

    out = jnp.concatenate([acc_ref[0:HEAD_DIM, :] / l_ref[0], acc_ref[HEAD_DIM:, :] / l_ref[1]], axis=0)
    o_ref[0] = out.T


def _prompt_attn(q, k, v, c_rows, moba, li):
    b, s, _ = q.shape
    t = MOBA_BLOCK
    nblk = s // t
    n_pairs = ATTN_HALF // LANES
    nbp = -(-nblk // SUBLANES) * SUBLANES
    aux = pltpu.VMEM((nbp, LANES), F32) if moba else pltpu.VMEM((2, s, LANES), F32)
    return pl.pallas_call(
        functools.partial(_prompt_attn_kernel, t=t, moba=moba),
        grid=(b, n_pairs, nblk),
        in_specs=[pl.BlockSpec((1, t, LANES), lambda i, p, j: (i, j, p)),
                  pl.BlockSpec((1, 1, LANES, s), lambda i, p, j: (li, i, p, 0)),
                  pl.BlockSpec((1, 1, LANES, s), lambda i, p, j: (li, i, p, 0)),
                  pl.BlockSpec((1, 1, 2, s), lambda i, p, j: (i, p, 0, 0))],
        out_specs=pl.BlockSpec((1, t, LANES), lambda i, p, j: (i, j, p)),
        out_shape=jax.ShapeDtypeStruct((b, s, ATTN_HALF), F32),
        scratch_shapes=[pltpu.VMEM((s, LANES), BF16), pltpu.VMEM((LANES, s), BF16), aux,
                        pltpu.VMEM((2, nblk, 1, t), F32),
                        pltpu.VMEM((2, 1, t), F32), pltpu.VMEM((2, 1, t), F32),
                        pltpu.VMEM((LANES, t), F32)],
        compiler_params=_cparams(3),
        name="moba_prompt" if moba else "fox_prompt",
    )(q, k, v, c_rows)


PAGES_PER_STEP = 16
DECAY_PAGES_PER_STEP = 32
ROWS = 32


def _decay_kernel(pt_ref, ln_ref, *refs, g):
    l_refs = refs[:g]
    dn_ref, d_ref, carry_ref = refs[g:]
    j = pl.program_id(1)
    later = (_iota((PAGE_SIZE, PAGE_SIZE), 0) > _iota((PAGE_SIZE, PAGE_SIZE), 1)).astype(F32).astype(BF16)

    def page(lt):
        d8 = _dot_split3(lt, later) + carry_ref[...]
        carry_ref[...] = carry_ref[...] + jnp.sum(lt, axis=1, keepdims=True)
        return d8

    @pl.when(j == 0)
    def _():
        carry_ref[...] = jnp.zeros_like(carry_ref)
        dn_ref[0] = page(ln_ref[0])

    for i in range(g):
        d_ref[0, g - 1 - i] = page(l_refs[i][0, 0])


def _decay(page_table, li, l_new, cache_lt):
    db = l_new.shape[0]
    n_pages = page_table.shape[1]
    g = min(DECAY_PAGES_PER_STEP, n_pages)
    steps = n_pages // g

    def page_map(i, n, j, pt):
        return (li, pt[n, n_pages - 1 - (j * g + i)], 0, 0)

    blk = (1, FOX_HEADS, PAGE_SIZE)
    grid_spec = pltpu.PrefetchScalarGridSpec(
        num_scalar_prefetch=1, grid=(db, steps),
        in_specs=[pl.BlockSpec(blk, lambda n, j, pt: (n, 0, 0))]
        + [pl.BlockSpec((1,) + blk, functools.partial(page_map, i)) for i in range(g)],
        out_specs=[pl.BlockSpec(blk, lambda n, j, pt: (n, 0, 0)),
                   pl.BlockSpec((1, g, FOX_HEADS, PAGE_SIZE), lambda n, j, pt: (n, steps - 1 - j, 0, 0))],
        scratch_shapes=[pltpu.VMEM((FOX_HEADS, 1), F32)])
    return pl.pallas_call(
        functools.partial(_decay_kernel, g=g),
        grid_spec=grid_spec,
        out_shape=[jax.ShapeDtypeStruct((db,) + blk[1:], F32),
                   jax.ShapeDtypeStruct((db, n_pages, FOX_HEADS, PAGE_SIZE), F32)],
        compiler_params=_cparams(2),
        name="fox_decay",
    )(page_table, l_new, *([cache_lt] * g))


def _block_diag_queries(q):
    t = q.shape[0]
    rep = jnp.concatenate([jnp.broadcast_to(q[i:i + 1], (FOX_HEADS, ATTN_HALF)) for i in range(t)], axis=0)
    keep = (_iota(rep.shape, 1) // HEAD_DIM) == (_iota(rep.shape, 0) % FOX_HEADS)
    return jnp.where(keep, rep, 0.0)


def _heads_to_tokens(o):
    keep = (_iota(o.shape, 1) // HEAD_DIM) == (_iota(o.shape, 0) % FOX_HEADS)
    o = jnp.where(keep, o, 0.0)
    t = o.shape[0] // FOX_HEADS
    return jnp.sum(o.reshape(t, FOX_HEADS, ATTN_HALF), axis=1)


def _page_bf16(ref):
    return ref[0, 0].astype(BF16)


def _fox_decode_kernel(pt_ref, q_ref, kn_ref, vn_ref, dn_ref, d_ref, *refs, g):
    k_refs, v_refs = refs[:g], refs[g:2 * g]
    o_ref, qb_ref, rc_ref, m_ref, l_ref, acc_ref = refs[2 * g:]
    j = pl.program_id(1)
    lane = _iota((ROWS, PAGE_SIZE), 1)
    tok = _iota((ROWS, PAGE_SIZE), 0) // FOX_HEADS
    per_row = lambda d8: jnp.concatenate([d8] * (ROWS // FOX_HEADS), axis=0)

    @pl.when(j == 0)
    def _():
        qb_ref[...] = (_block_diag_queries(q_ref[0]) * ATTN_SCALE).astype(BF16)
        d = per_row(dn_ref[0])
        rc_ref[...] = jnp.sum(jnp.where(lane == tok, d, 0.0), axis=1, keepdims=True)
        m_ref[...] = jnp.full(m_ref.shape, NEG_INF, F32)
        l_ref[...] = jnp.zeros_like(l_ref)
        s = _dot(qb_ref[...], kn_ref[0].astype(BF16)) + d - rc_ref[...]
        p, _ = _softmax_step(s, lane <= tok, 0, m_ref, l_ref, 1)
        acc_ref[...] = _dot_nt(p.astype(BF16), vn_ref[0].astype(BF16))

    scores = [_dot(qb_ref[...], _page_bf16(k_refs[i])) + per_row(d_ref[0, i]) - rc_ref[...] for i in range(g)]
    m_old = m_ref[0]
    m_new = functools.reduce(jnp.maximum, [jnp.max(s, axis=1, keepdims=True) for s in scores] + [m_old])
    alpha = jnp.exp(m_old - m_new)
    l_new = alpha * l_ref[0]
    acc = acc_ref[...] * alpha
    for i, s in enumerate(scores):
        p = jnp.exp(s - m_new)
        l_new = l_new + jnp.sum(p, axis=1, keepdims=True)
        acc = acc + _dot_nt(p.astype(BF16), _page_bf16(v_refs[i]))
    m_ref[0] = m_new
    l_ref[0] = l_new
    acc_ref[...] = acc

    @pl.when(j == pl.num_programs(1) - 1)
    def _():
        o_ref[0] = _heads_to_tokens(acc_ref[...] / l_ref[0])


def _page_specs(li, g):
    def page_map(i, n, j, pt):
        return (li, pt[n, j * g + i], 0, 0)

    return [pl.BlockSpec((1, 1, ATTN_HALF, PAGE_SIZE), functools.partial(page_map, i)) for i in range(g)]


def _fox_decode(page_table, li, q, k_new, v_new, d_new, d_past, cache_k, cache_v):
    db, t, _ = q.shape
    n_pages = page_table.shape[1]
    g = PAGES_PER_STEP
    sample = lambda n, j, pt: (n, 0, 0)
    new_rows = pl.BlockSpec((1, ATTN_HALF, PAGE_SIZE), sample)
    in_specs = [pl.BlockSpec((1, t, ATTN_HALF), sample), new_rows, new_rows,
                pl.BlockSpec((1, FOX_HEADS, PAGE_SIZE), sample),
                pl.BlockSpec((1, g, FOX_HEADS, PAGE_SIZE), lambda n, j, pt: (n, j, 0, 0))]
    in_specs += _page_specs(li, g) * 2
    grid_spec = pltpu.PrefetchScalarGridSpec(
        num_scalar_prefetch=1, grid=(db, n_pages // g), in_specs=in_specs,
        out_specs=pl.BlockSpec((1, t, ATTN_HALF), sample),
        scratch_shapes=[pltpu.VMEM((ROWS, ATTN_HALF), BF16), pltpu.VMEM((ROWS, 1), F32),
                        pltpu.VMEM((1, ROWS, 1), F32), pltpu.VMEM((1, ROWS, 1), F32),
                        pltpu.VMEM((ROWS, ATTN_HALF), F32)])
    return pl.pallas_call(
        functools.partial(_fox_decode_kernel, g=g),
        grid_spec=grid_spec,
        out_shape=jax.ShapeDtypeStruct((db, t, ATTN_HALF), F32),
        compiler_params=_cparams(2),
        name="fox_decode",
    )(page_table, q, k_new, v_new, d_new, d_past, *([cache_k] * g), *([cache_v] * g))


def _moba_decode_kernel(pt_ref, q_ref, qt_ref, kn_ref, vn_ref, *refs, g, n_blocks):
    k_refs, v_refs = refs[:g], refs[g:2 * g]
    o_ref, qb_ref, qft_ref, mb_ref, lb_ref, gb_ref, ob_ref, m_ref, l_ref = refs[2 * g:]
    j = pl.program_id(1)
    lane = _iota((ROWS, LANES), 1)
    row = _iota((ROWS, LANES), 0)
    tok = row // MOBA_HEADS
    pages_per_block = MOBA_BLOCK // PAGE_SIZE

    @pl.when(j == 0)
    def _():
        qb_ref[...] = (_block_diag_queries(q_ref[0]) * ATTN_SCALE).astype(BF16)
        qt = qt_ref[0]
        r_id = _iota(qt.shape, 1)
        c_id = _iota(qt.shape, 0)
        rep = jnp.zeros(qt.shape, F32)
        for t in range(ROWS // MOBA_HEADS):
            rep = jnp.where((r_id // MOBA_HEADS) == t, qt[:, t:t + 1], rep)
        qft_ref[...] = jnp.where(((c_id // HEAD_DIM) == (r_id % MOBA_HEADS)) & (r_id < ROWS), rep, 0.0)
        mb_ref[...] = jnp.zeros_like(mb_ref)
        lb_ref[...] = jnp.zeros_like(lb_ref)
        gb_ref[...] = jnp.zeros_like(gb_ref)

    blocks_per_step = g // pages_per_block
    scores = [_dot(qb_ref[...], _page_bf16(kp)) for kp in k_refs]
    mb, lb, gb = mb_ref[...], lb_ref[...], gb_ref[...]
    for bl in range(blocks_per_step):
        blk = j * blocks_per_step + bl
        pages = range(bl * pages_per_block, (bl + 1) * pages_per_block)
        kmean = jnp.sum(sum(k_refs[i][0, 0] for i in pages), axis=1, keepdims=True) * (1.0 / MOBA_BLOCK)
        gate_row = jnp.sum(qft_ref[...] * kmean, axis=0, keepdims=True)
        gate = jnp.sum(jnp.where(row == lane, gate_row, 0.0), axis=1, keepdims=True)
        m_b = functools.reduce(jnp.maximum, [jnp.max(scores[i], axis=1, keepdims=True) for i in pages])
        ps = [jnp.exp(scores[i] - m_b) for i in pages]
        l_b = sum(jnp.sum(p, axis=1, keepdims=True) for p in ps)
        ob_ref[blk] = sum(_dot_nt(p.astype(BF16), _page_bf16(v_refs[i])) for p, i in zip(ps, pages))
        here = lane == blk
        mb = jnp.where(here, m_b, mb)
        lb = jnp.where(here, l_b, lb)
        gb = jnp.where(here, gate, gb)
    mb_ref[...], lb_ref[...], gb_ref[...] = mb, lb, gb

    @pl.when(j == pl.num_programs(1) - 1)
    def _():
        m_ref[...] = jnp.full(m_ref.shape, NEG_INF, F32)
        l_ref[...] = jnp.zeros_like(l_ref)
        s_own = _dot(qb_ref[...], kn_ref[0].astype(BF16))
        p_own, _ = _softmax_step(s_own, lane <= tok, 0, m_ref, l_ref, 1)
        o_own = _dot_nt(p_own.astype(BF16), vn_ref[0].astype(BF16))
        m_own, l_own = m_ref[0], l_ref[0]
        valid = lane < n_blocks
        gate = jnp.where(valid, gb_ref[...], NEG_INF)
        rank = jnp.zeros((ROWS, LANES), jnp.int32)
        for b in range(n_blocks):
            gj = gate[:, b:b + 1]
            rank = rank + ((gj > gate) | ((gj == gate) & (lane > b))).astype(jnp.int32)
        sel = (rank < MOBA_TOPK) & valid
        m_all = jnp.maximum(jnp.max(jnp.where(sel, mb_ref[...], NEG_INF), axis=1, keepdims=True), m_own)
        wgt = jnp.where(sel, jnp.exp(jnp.where(sel, mb_ref[...] - m_all, 0.0)), 0.0)
        w_own = jnp.exp(m_own - m_all)
        den = jnp.sum(wgt * lb_ref[...], axis=1, keepdims=True) + w_own * l_own
        num = w_own * o_own
        for b in range(n_blocks):
            num = num + wgt[:, b:b + 1] * ob_ref[b]
        o_ref[0] = _heads_to_tokens(num / den)


def _moba_decode(page_table, li, q, q_t, k_new, v_new, cache_k, cache_v):
    db, t, _ = q.shape
    n_pages = page_table.shape[1]
    g = PAGES_PER_STEP
    n_blocks = n_pages * PAGE_SIZE // MOBA_BLOCK
    sample = lambda n, j, pt: (n, 0, 0)
    new_rows = pl.BlockSpec((1, ATTN_HALF, PAGE_SIZE), sample)
    tokens = pl.BlockSpec((1, t, ATTN_HALF), sample)
    grid_spec = pltpu.PrefetchScalarGridSpec(
        num_scalar_prefetch=1, grid=(db, n_pages // g),
        in_specs=[tokens, new_rows, new_rows, new_rows] + _page_specs(li, g) * 2, out_specs=tokens,
        scratch_shapes=[pltpu.VMEM((ROWS, ATTN_HALF), BF16), pltpu.VMEM((ATTN_HALF, LANES), F32),
                        pltpu.VMEM((ROWS, LANES), F32), pltpu.VMEM((ROWS, LANES), F32),
                        pltpu.VMEM((ROWS, LANES), F32), pltpu.VMEM((n_blocks, ROWS, ATTN_HALF), F32),
                        pltpu.VMEM((1, ROWS, 1), F32), pltpu.VMEM((1, ROWS, 1), F32)])
    return pl.pallas_call(
        functools.partial(_moba_decode_kernel, g=g, n_blocks=n_blocks),
        grid_spec=grid_spec,
        out_shape=jax.ShapeDtypeStruct((db, t, ATTN_HALF), F32),
        compiler_params=_cparams(2),
        name="moba_decode",
    )(page_table, q, q_t, k_new, v_new, *([cache_k] * g), *([cache_v] * g))


def _pool_kernel(u_ref, ctx_ref, w_ref, sc_ref, o_ref, ext_ref, *, tm, pos0):
    i = pl.program_id(1)

    @pl.when(i == 0)
    def _():
        ext_ref[0:POOL_HALO, :] = ctx_ref[0]

    @pl.when(i > 0)
    def _():
        ext_ref[0:POOL_HALO, :] = ext_ref[tm:tm + POOL_HALO, :]

    ext_ref[POOL_HALO:, :] = u_ref[0]
    pos = pos0 + i * tm + _iota((tm, 1), 0)
    for gi, win in enumerate(POOL_WINDOWS):
        cols = slice(gi * POOL_GROUP_WIDTH, (gi + 1) * POOL_GROUP_WIDTH)
        x = ext_ref[:, cols]
        tot = x
        shift = 1
        while shift < win:
            tot = tot + pltpu.roll(tot, shift, 0)
            shift *= 2
        cnt = jnp.minimum(win, pos + 1).astype(F32)
        d = tot[POOL_HALO:] / cnt - x[POOL_HALO:]
        y = _dot(d.astype(BF16), w_ref[gi])
        o_ref[0, :, cols] = y * sc_ref[:, cols]


def _pool_mix(u, ctx, w_pool, pool_scale, pos0):
    n, l, _ = u.shape
    tm = min(512, l)
    return pl.pallas_call(
        functools.partial(_pool_kernel, tm=tm, pos0=pos0),
        grid=(n, l // tm),
        in_specs=[pl.BlockSpec((1, tm, POOL_WIDTH), lambda b, i: (b, i, 0)),
                  pl.BlockSpec((1, POOL_HALO, POOL_WIDTH), lambda b, i: (b, 0, 0)),
                  pl.BlockSpec((len(POOL_WINDOWS), POOL_GROUP_WIDTH, POOL_GROUP_WIDTH), lambda b, i: (0, 0, 0)),
                  pl.BlockSpec((1, POOL_WIDTH), lambda b, i: (0, 0))],
        out_specs=pl.BlockSpec((1, tm, POOL_WIDTH), lambda b, i: (b, i, 0)),
        out_shape=jax.ShapeDtypeStruct((n, l, POOL_WIDTH), F32),
        scratch_shapes=[pltpu.VMEM((POOL_HALO + tm, POOL_WIDTH), F32)],
        compiler_params=_cparams(2),
        name="pool_mix",
    )(u, ctx, w_pool, pool_scale)


def _gdn_prep_kernel(x_ref, ctx_ref, ab_ref, cw_ref, alog_ref, dtb_ref, q_ref, k_ref, v_ref, gb_ref, ext_ref, *, tm):
    i = pl.program_id(1)

    @pl.when(i == 0)
    def _():
        ext_ref[0:CONV_HALO, :] = ctx_ref[0]

    @pl.when(i > 0)
    def _():
        ext_ref[0:CONV_HALO, :] = ext_ref[tm:tm + CONV_HALO, :]

    ext_ref[CONV_HALO:, :] = x_ref[0]
    for part, out_ref in enumerate((q_ref, k_ref, v_ref)):
        for h in range(GDN_HEADS):
            c0 = part * GDN_WIDTH + h * GDN_HEAD_DIM
            x = ext_ref[:, c0:c0 + GDN_HEAD_DIM]
            y = x[CONV_HALO:] * cw_ref[CONV_WIDTH - 1:CONV_WIDTH, c0:c0 + GDN_HEAD_DIM]
            for tap in range(CONV_WIDTH - 1):
                shifted = pltpu.roll(x, CONV_CTX - tap, 0)[CONV_HALO:]
                y = y + shifted * cw_ref[tap:tap + 1, c0:c0 + GDN_HEAD_DIM]
            y = y * _sigmoid(y)
            if part < 2:
                y = y * lax.rsqrt(jnp.sum(y * y, axis=-1, keepdims=True) + NORM_EPS)
            if part == 0:
                y = y * GDN_SCALE
            out_ref[0, :, h * GDN_HEAD_DIM:(h + 1) * GDN_HEAD_DIM] = y
    ab = ab_ref[0]
    g = -jnp.exp(alog_ref[...]) * _softplus(ab + dtb_ref[...])
    beta = _sigmoid(ab)
    gb_ref[0] = jnp.where(_iota(ab.shape, 1) < GDN_HEADS, g, beta)


def _gdn_prep(qkv, ctx, ab, conv_w, a_log_row, dt_bias_row):
    n, l, width = qkv.shape
    tm = min(256, l)
    outs = [jax.ShapeDtypeStruct((n, l, GDN_WIDTH), F32)] * 3 + [jax.ShapeDtypeStruct((n, l, LANES), F32)]
    row = lambda b, i: (b, i, 0)
    fixed = lambda b, i: (0, 0)
    return pl.pallas_call(
        functools.partial(_gdn_prep_kernel, tm=tm),
        grid=(n, l // tm),
        in_specs=[pl.BlockSpec((1, tm, width), row),
                  pl.BlockSpec((1, CONV_HALO, width), lambda b, i: (b, 0, 0)),
                  pl.BlockSpec((1, tm, LANES), row),
                  pl.BlockSpec((CONV_WIDTH, width), fixed),
                  pl.BlockSpec((1, LANES), fixed),
                  pl.BlockSpec((1, LANES), fixed)],
        out_specs=[pl.BlockSpec((1, tm, GDN_WIDTH), row)] * 3 + [pl.BlockSpec((1, tm, LANES), row)],
        out_shape=outs,
        scratch_shapes=[pltpu.VMEM((CONV_HALO + tm, width), F32)],
        compiler_params=_cparams(2),
        name="gdn_prep",
    )(qkv, ctx, ab, conv_w, a_log_row, dt_bias_row)


def _gdn_chunk_kernel(q_ref, k_ref, v_ref, gb_ref, u_ref, w_ref, qg_ref, kd_ref, at_ref, eg_ref, *, c, chunks):
    ri = _iota((c, c), 0)
    ci = _iota((c, c), 1)
    eye = ri == ci
    incl = ri >= ci
    strict = ri > ci
    eye_f = eye.astype(F32)
    levels = max(1, (c - 1).bit_length())
    pairs = [(ch, h) for ch in range(chunks) for h in range(GDN_HEADS)]
    rows = lambda ch: slice(ch * c, (ch + 1) * c)
    cols = lambda h: slice(h * GDN_HEAD_DIM, (h + 1) * GDN_HEAD_DIM)
    ks = [k_ref[0, rows(ch), cols(h)] for ch, h in pairs]
    g_cols = [gb_ref[0, rows(ch), h:h + 1] for ch, h in pairs]
    betas = [gb_ref[0, rows(ch), GDN_HEADS + h:GDN_HEADS + h + 1] for ch, h in pairs]
    gc_cols, decays = [], []
    for g_col in g_cols:
        g_row = jnp.sum(jnp.where(eye, g_col, 0.0), axis=0, keepdims=True)
        gc_col = jnp.sum(jnp.where(incl, g_row, 0.0), axis=1, keepdims=True)
        gc_row = jnp.sum(jnp.where(ri <= ci, g_col, 0.0), axis=0, keepdims=True)
        gc_cols.append(gc_col)
        decays.append(jnp.where(incl, jnp.exp(jnp.where(incl, gc_col - gc_row, 0.0)), 0.0))
    kbetas = [k * beta for k, beta in zip(ks, betas)]
    a_mats = [jnp.where(strict, _mm_nt(kb, k) * d, 0.0) for kb, k, d in zip(kbetas, ks, decays)]
    ts = [eye_f - a for a in a_mats]
    npows = [_mm(a, a) for a in a_mats]
    for lvl in range(1, levels):
        ts = [t + _mm(t, npow) for t, npow in zip(ts, npows)]
        if lvl < levels - 1:
            npows = [_mm(npow, npow) for npow in npows]
    egs = [jnp.exp(gc) for gc in gc_cols]
    for idx, (ch, h) in enumerate(pairs):
        r, cl = rows(ch), cols(h)
        q, k, gc = q_ref[0, r, cl], ks[idx], gc_cols[idx]
        u_ref[0, r, cl] = _mm(ts[idx], v_ref[0, r, cl] * betas[idx])
        w_ref[0, r, cl] = _mm(ts[idx], kbetas[idx] * egs[idx])
        qg_ref[0, r, cl] = q * egs[idx]
        kd_ref[0, r, cl] = k * jnp.exp(gc[c - 1:c, :] - gc)
        at_ref[0, r, h * c:(h + 1) * c] = jnp.where(incl, _mm_nt(q, k) * decays[idx], 0.0)
    lane = _iota((c, LANES), 1)
    for ch in range(chunks):
        eg_full = jnp.zeros((c, LANES), F32)
        for h in range(GDN_HEADS):
            eg_full = jnp.where(lane == h, egs[ch * GDN_HEADS + h], eg_full)
        eg_ref[0, rows(ch), :] = eg_full


def _gdn_chunks(q, k, v, gb, c):
    n, l, _ = q.shape
    chunks = max(1, min(4, l // c))
    tm = chunks * c
    row = lambda b, i: (b, i, 0)
    big = jax.ShapeDtypeStruct((n, l, GDN_WIDTH), F32)
    outs = [big] * 4 + [jax.ShapeDtypeStruct((n, l, GDN_HEADS * c), F32), jax.ShapeDtypeStruct((n, l, LANES), F32)]
    return pl.pallas_call(
        functools.partial(_gdn_chunk_kernel, c=c, chunks=chunks),
        grid=(n, l // tm),
        in_specs=[pl.BlockSpec((1, tm, GDN_WIDTH), row)] * 3 + [pl.BlockSpec((1, tm, LANES), row)],
        out_specs=[pl.BlockSpec((1, tm, GDN_WIDTH), row)] * 4
        + [pl.BlockSpec((1, tm, GDN_HEADS * c), row), pl.BlockSpec((1, tm, LANES), row)],
        out_shape=outs,
        compiler_params=_cparams(2),
        name="gdn_chunks",
    )(q, k, v, gb)


SCAN_BATCH = 4


def _gdn_scan_kernel(u_ref, w_ref, qg_ref, kd_ref, at_ref, eg_ref, z_ref, s0_ref, nw_ref, o_ref, sout_ref, s_ref, *, c):
    i = pl.program_id(1)

    @pl.when(i == 0)
    def _():
        s_ref[...] = s0_ref[...]

    pairs = [(b, h) for b in range(s_ref.shape[0]) for h in range(GDN_HEADS)]
    cols = lambda h: slice(h * GDN_HEAD_DIM, (h + 1) * GDN_HEAD_DIM)
    states = [s_ref[b, h] for b, h in pairs]
    v_news = [u_ref[b, :, cols(h)] - _mm(w_ref[b, :, cols(h)], s) for (b, h), s in zip(pairs, states)]
    outs = [_mm(qg_ref[b, :, cols(h)], s) + _mm(at_ref[b, :, h * c:(h + 1) * c], v)
            for (b, h), s, v in zip(pairs, states, v_news)]
    for (b, h), s, v, o in zip(pairs, states, v_news, outs):
        s_ref[b, h] = s * eg_ref[b, c - 1:c, h:h + 1] + _mm_tn(kd_ref[b, :, cols(h)], v)
        o = o * lax.rsqrt(jnp.mean(o * o, axis=-1, keepdims=True) + NORM_EPS) * nw_ref[...]
        z = z_ref[b, :, cols(h)]
        o_ref[b, :, cols(h)] = o * (z * _sigmoid(z))

    @pl.when(i == pl.num_programs(1) - 1)
    def _():
        sout_ref[...] = s_ref[...]


def _gdn_scan(u, w, qg, kd, at, eg, z, s0, norm_w, c):
    n, l, _ = u.shape
    nb = SCAN_BATCH if n % SCAN_BATCH == 0 else 1
    row = lambda b, i: (b, i, 0)
    state = lambda b, i: (b, 0, 0, 0)
    st_shape = (nb, GDN_HEADS, GDN_HEAD_DIM, GDN_HEAD_DIM)
    return pl.pallas_call(
        functools.partial(_gdn_scan_kernel, c=c),
        grid=(n // nb, l // c),
        in_specs=[pl.BlockSpec((nb, c, GDN_WIDTH), row)] * 4
        + [pl.BlockSpec((nb, c, GDN_HEADS * c), row), pl.BlockSpec((nb, c, LANES), row),
           pl.BlockSpec((nb, c, GDN_WIDTH), row), pl.BlockSpec(st_shape, state),
           pl.BlockSpec((1, GDN_HEAD_DIM), lambda b, i: (0, 0))],
        out_specs=[pl.BlockSpec((nb, c, GDN_WIDTH), row), pl.BlockSpec(st_shape, state)],
        out_shape=[jax.ShapeDtypeStruct((n, l, GDN_WIDTH), F32),
                   jax.ShapeDtypeStruct((n,) + st_shape[1:], F32)],
        scratch_shapes=[pltpu.VMEM(st_shape, F32)],
        compiler_params=_cparams(2),
        name="gdn_scan",
    )(u, w, qg, kd, at, eg, z, s0, norm_w)


def _pad_lanes(x, width=LANES):
    return jnp.pad(x, [(0, 0)] * (x.ndim - 1) + [(0, width - x.shape[-1])])


def _even_weights(w_in, b_f):
    fq, fk, fv, fl, mq, mk, mv = jnp.split(
        w_in, [ATTN_HALF, 2 * ATTN_HALF, 3 * ATTN_HALF, 3 * ATTN_HALF + FOX_HEADS,
               4 * ATTN_HALF + FOX_HEADS, 5 * ATTN_HALF + FOX_HEADS], axis=1)
    w_main = jnp.concatenate([fq, fk, fv, mq, mk, mv], axis=1).astype(BF16)
    return w_main, _pad_lanes(fl).astype(BF16), _pad_lanes(b_f[None, :])


def _odd_weights(w_in):
    u, qkv, a, b, z = jnp.split(
        w_in, [POOL_WIDTH, POOL_WIDTH + 3 * GDN_WIDTH, POOL_WIDTH + 3 * GDN_WIDTH + GDN_HEADS,
               POOL_WIDTH + 3 * GDN_WIDTH + 2 * GDN_HEADS], axis=1)
    w_main = jnp.concatenate([u, qkv, z], axis=1).astype(BF16)
    return w_main, _pad_lanes(jnp.concatenate([a, b], axis=1)).astype(BF16)


def _odd_mixer(h, n, l, pool_ctx, conv_ctx, s0, pos0, chunk, w_main, w_small, conv_w, a_log_row, dt_bias_row,
               norm_w, w_pool, pool_scale):
    u, qkv, z, ab = _proj(h, w_main, w_small, jnp.zeros((1, LANES), F32),
                          (POOL_WIDTH, 3 * GDN_WIDTH, GDN_WIDTH), False)
    u3, qkv3, z3, ab3 = (t.reshape(n, l, -1) for t in (u, qkv, z, ab))
    lp = -(-l // SUBLANES) * SUBLANES
    padr = lambda t: jnp.pad(t, ((0, 0), (0, lp - l), (0, 0)))
    ctx_p = jnp.pad(pool_ctx, ((0, 0), (POOL_HALO - POOL_CTX, 0), (0, 0)))
    ctx_c = jnp.pad(conv_ctx, ((0, 0), (CONV_HALO - CONV_CTX, 0), (0, 0)))
    y_pool = _pool_mix(padr(u3), ctx_p, w_pool, pool_scale, pos0)[:, :l]
    q, k, v, gb = _gdn_prep(padr(qkv3), ctx_c, padr(ab3), conv_w, a_log_row, dt_bias_row)
    if lp != l:
        gb = gb * (jnp.arange(lp) < l).astype(F32)[None, :, None]
    if chunk % SUBLANES:
        assert chunk == l
        chunk = lp
    c = chunk
    uu, ww, qg, kd, at, eg = _gdn_chunks(q, k, v, gb, c)
    o, s_new = _gdn_scan(uu, ww, qg, kd, at, eg, padr(z3), s0, norm_w, c)
    new_pool = jnp.concatenate([pool_ctx, u3], axis=1)[:, -POOL_CTX:]
    new_conv = jnp.concatenate([conv_ctx, qkv3], axis=1)[:, -CONV_CTX:]
    return (y_pool.reshape(n * l, POOL_WIDTH), o[:, :l].reshape(n * l, GDN_WIDTH), new_pool, new_conv, s_new)


def kernel(x_prompt, x_sample, cache_fox_k, cache_fox_v, cache_fox_logf, cache_moba_k, cache_moba_v, state_pool, state_conv, state_gdn, page_table, w_in_even, b_forget, w_out_even, w_in_odd, conv_w, a_log, dt_bias, gdn_norm_w, w_pool, pool_scale, w_out_odd, ln1_g, ln1_b, ln2_g, ln2_b, w_gate_up, w_down):
    B, S, _ = x_prompt.shape
    DB, T, _ = x_sample.shape
    P = page_table.shape[1] * PAGE_SIZE
    hp = x_prompt.reshape(B * S, D_MODEL)
    hs = x_sample.reshape(DB * T, D_MODEL)

    assert T * FOX_HEADS == ROWS and P % MOBA_BLOCK == 0
    cfl_t = jnp.swapaxes(cache_fox_logf, 2, 3)
    paged_t = lambda c: jnp.transpose(c, (0, 1, 3, 4, 2)).reshape(c.shape[0], c.shape[1], ATTN_HALF, PAGE_SIZE)
    cfk, cfv, cmk, cmv = paged_t(cache_fox_k), paged_t(cache_fox_v), paged_t(cache_moba_k), paged_t(cache_moba_v)
    new_t = lambda t: jnp.pad(jnp.swapaxes(t.reshape(DB, T, ATTN_HALF), 1, 2), ((0, 0), (0, 0), (0, PAGE_SIZE - T)))

    outs_p = {k: [] for k in ("fk", "fv", "fl", "mk", "mv", "pool", "conv", "gdn")}
    outs_s = {k: [] for k in ("fk", "fv", "fl", "mk", "mv", "pool", "conv", "gdn")}
    widths_even = (ATTN_HALF,) * 6
    n_even = cache_fox_k.shape[0]
    kv_t = ()
    for layer in range(DEPTH):
        li = layer // 2
        row = lambda a: a[layer][None, :]
        if layer % 2 == 0:
            w_main, w_fl, b_fl = _even_weights(w_in_even[li], b_forget[li])
            w_out = w_out_even[li].astype(BF16)
            fq, fk, fv, mq, mk, mv, fl = _proj(hp, w_main, w_fl, b_fl, widths_even, True,
                                               transposed=(False, True, True, False, True, True), batch=B,
                                               layer=(li, n_even), carried=kv_t)
            kv_t = (fk, fv, mk, mv)
            b3 = lambda t: t.reshape(B, S, ATTN_HALF)
            logf = fl[:, :FOX_HEADS].reshape(B, S, FOX_HEADS)
            c_rows = _cumsum_rows(jnp.swapaxes(logf, 1, 2)).reshape(B, FOX_HEADS // 2, 2, S)
            fox = _prompt_attn(b3(fq), fk, fv, c_rows, False, li)
            moba = _prompt_attn(b3(mq), mk, mv, c_rows, True, li)
            a1_p, a2_p = fox.reshape(B * S, ATTN_HALF), moba.reshape(B * S, ATTN_HALF)
            hd = lambda t, n, l: t.reshape(n, l, FOX_HEADS, HEAD_DIM)
            outs_p["fl"].append(logf)
            fq, fk, fv, mq, mk, mv, fl = _proj(hs, w_main, w_fl, b_fl, widths_even, True)
            logf = fl[:, :FOX_HEADS].reshape(DB, T, FOX_HEADS)
            l_new = _pad_lanes(jnp.swapaxes(logf, 1, 2), PAGE_SIZE)
            d_new, d_past = _decay(page_table, li, l_new, cfl_t)
            tok = lambda t: t.reshape(DB, T, ATTN_HALF)
            fox = _fox_decode(page_table, li, tok(fq), new_t(fk), new_t(fv), d_new, d_past, cfk, cfv)
            moba = _moba_decode(page_table, li, tok(mq), new_t(mq), new_t(mk), new_t(mv), cmk, cmv)
            a1_s, a2_s = fox.reshape(DB * T, ATTN_HALF), moba.reshape(DB * T, ATTN_HALF)
            for key, val in (("fk", hd(fk, DB, T)), ("fv", hd(fv, DB, T)), ("fl", logf),
                             ("mk", hd(mk, DB, T)), ("mv", hd(mv, DB, T))):
                outs_s[key].append(val)
            w1, w2 = w_out[:ATTN_HALF], w_out[ATTN_HALF:]
        else:
            w_main, w_small = _odd_weights(w_in_odd[li])
            w_out = w_out_odd[li].astype(BF16)
            shared = (w_main, w_small, conv_w[li], _pad_lanes(a_log[li][None, :]), _pad_lanes(dt_bias[li][None, :]),
                      gdn_norm_w[li][None, :], w_pool[li].astype(BF16), pool_scale[li][None, :])
            a1_p, a2_p, sp, sc, sg = _odd_mixer(
                hp, B, S, jnp.zeros((B, POOL_CTX, POOL_WIDTH), F32), jnp.zeros((B, CONV_CTX, 3 * GDN_WIDTH), F32),
                jnp.zeros((B, GDN_HEADS, GDN_HEAD_DIM, GDN_HEAD_DIM), F32), 0, GDN_CHUNK, *shared)
            outs_p["pool"].append(sp); outs_p["conv"].append(sc); outs_p["gdn"].append(sg)
            a1_s, a2_s, sp, sc, sg = _odd_mixer(
                hs, DB, T, state_pool[li], state_conv[li], state_gdn[li], P, T, *shared)
            outs_s["pool"].append(sp); outs_s["conv"].append(sc); outs_s["gdn"].append(sg)
            w1, w2 = w_out[:POOL_WIDTH], w_out[POOL_WIDTH:]
        wgu, wdn = w_gate_up[layer].astype(BF16), w_down[layer].astype(BF16)
        hp = _outproj_ln(a1_p, a2_p, hp, w1, w2, row(ln1_g), row(ln1_b))
        hp = _ffn_ln(hp, wgu, wdn, row(ln2_g), row(ln2_b))
        hs = _outproj_ln(a1_s, a2_s, hs, w1, w2, row(ln1_g), row(ln1_b))
        hs = _ffn_ln(hs, wgu, wdn, row(ln2_g), row(ln2_b))

    st = jnp.stack
    keys = ("fk", "fv", "fl", "mk", "mv", "pool", "conv", "gdn")
    heads_last = lambda t: jnp.transpose(t.reshape(n_even, B, FOX_HEADS, HEAD_DIM, S), (0, 1, 4, 2, 3))
    prompt_kv = dict(zip(("fk", "fv", "mk", "mv"), (heads_last(t) for t in kv_t)))
    return ((hp.reshape(B, S, D_MODEL), hs.reshape(DB, T, D_MODEL))
            + tuple(prompt_kv[k] if k in prompt_kv else st(outs_p[k]) for k in keys)
            + tuple(st(outs_s[k]) for k in keys))
```
